```python
import math
import jax
import jax.numpy as jnp
from jax import lax
import numpy as np

D_MODEL = 1024
BATCH = 16
SEQ = 256
DEPTH = 2
DEC_BATCH = 8
DEC_SEQ = 2048
PAST_LEN = 256

GRID_W = 64
N_MIXERS = 2
N_CONV_LAYERS = (DEPTH + 1) // 2
N_SSD_LAYERS = DEPTH // 2
CONV_WIDTH = 2 * D_MODEL
CONV_K = 31
SSD_INNER = 2 * D_MODEL
SSD_HEAD_DIM = 64
SSD_HEADS = SSD_INNER // SSD_HEAD_DIM
SSD_GROUPS = 4
D_STATE = 128
SSD_CONV_K = 5
CHUNK = 128
XBC_DIM = SSD_INNER + 2 * SSD_GROUPS * D_STATE
SSD_PROJ_DIM = SSD_INNER + XBC_DIM + 2 * SSD_HEADS
EPS = 1e-6

kernel_name = 'conv_ssd_hybrid_diffusion_step'


def rms_norm(x, g):
    xf = x.astype(jnp.float32)
    y = xf * lax.rsqrt(jnp.mean(xf * xf, axis=-1, keepdims=True) + EPS)
    return (y * g.astype(jnp.float32)).astype(x.dtype)


def layer_norm(x, g, b):
    xf = x.astype(jnp.float32)
    mu = jnp.mean(xf, axis=-1, keepdims=True)
    xc = xf - mu
    var = jnp.mean(xc * xc, axis=-1, keepdims=True)
    y = xc * lax.rsqrt(var + EPS) * g.astype(jnp.float32) + b.astype(jnp.float32)
    return y.astype(x.dtype)


def depthwise_conv(x, w, b):
    k = w.shape[0]
    y = lax.conv_general_dilated(x, w[:, None, :], window_strides=(1,), padding=[(k // 2, k // 2)],
                                 dimension_numbers=('NWC', 'WIO', 'NWC'), feature_group_count=x.shape[-1])
    return y + b


def adaln_params(cond, w, b):
    m = jax.nn.silu(cond) @ w + b
    return jnp.split(m, 3, axis=-1)


def conformer_branch(h, on_grid, w_in, b_in, w_dw, b_dw, ln_g, ln_b, w_out):
    u = h @ w_in + b_in
    a, a_gate, z = jnp.split(u, 3, axis=-1)
    v = a * jax.nn.sigmoid(a_gate)
    nb, length, e = v.shape
    if on_grid:
        rows = length // GRID_W
        v = v.reshape(nb * rows, GRID_W, e)
    v = depthwise_conv(v, w_dw, b_dw).reshape(nb, length, e)
    v = jax.nn.silu(layer_norm(v, ln_g, ln_b))
    return (v * jax.nn.silu(z)) @ w_out


def ssd_scan(x, dt, a_neg, bm, cm, h0):
    f32 = jnp.float32
    b, l, nh, p = x.shape
    g, n = bm.shape[2], bm.shape[3]
    r = nh // g
    c = l // CHUNK
    xdt = (x.astype(f32) * dt[..., None]).reshape(b, c, CHUNK, g, r, p)
    la = jnp.moveaxis((dt * a_neg).reshape(b, c, CHUNK, g, r), 2, -1)
    a_cum = jnp.cumsum(la, axis=-1)
    bc = bm.astype(f32).reshape(b, c, CHUNK, g, n)
    cc = cm.astype(f32).reshape(b, c, CHUNK, g, n)
    lower = jnp.tril(jnp.ones((CHUNK, CHUNK), dtype=bool))
    seg = a_cum[..., :, None] - a_cum[..., None, :]
    decay_mat = jnp.exp(jnp.where(lower, seg, -jnp.inf))
    cb = jnp.einsum('bclgn,bcsgn->bcgls', cc, bc)
    y_diag = jnp.einsum('bcgrls,bcsgrp->bclgrp', cb[:, :, :, None] * decay_mat, xdt)
    decay_to_end = jnp.exp(a_cum[..., -1:] - a_cum)
    chunk_states = jnp.einsum('bcsgn,bcgrs,bcsgrp->bcgrpn', bc, decay_to_end, xdt)
    chunk_decay = jnp.exp(a_cum[..., -1])

    def step(state, inp):
        dec, st = inp
        return state * dec[..., None, None] + st, state

    h0g = h0.astype(f32).reshape(b, g, r, p, n)
    final, prev = lax.scan(step, h0g, (jnp.moveaxis(chunk_decay, 1, 0), jnp.moveaxis(chunk_states, 1, 0)))
    prev = jnp.moveaxis(prev, 0, 1)
    y_off = jnp.einsum('bclgn,bcgrpn,bcgrl->bclgrp', cc, prev, jnp.exp(a_cum))
    y = (y_diag + y_off).reshape(b, l, nh, p)
    return y, final.reshape(b, nh, p, n)


def ssd_branch(h, h0_f, h0_b, w_in, w_conv, b_conv, dt_bias_f, dt_bias_b, a_log_f, a_log_b, d_skip, norm_g, w_out):
    f32 = jnp.float32
    b, l, _ = h.shape
    u = h @ w_in
    z, xbc, dt_raw = jnp.split(u, [SSD_INNER, SSD_INNER + XBC_DIM], axis=-1)
    xbc = jax.nn.silu(depthwise_conv(xbc, w_conv, b_conv))
    xs, bm, cm = jnp.split(xbc, [SSD_INNER, SSD_INNER + SSD_GROUPS * D_STATE], axis=-1)
    xs = xs.reshape(b, l, SSD_HEADS, SSD_HEAD_DIM)
    bm = bm.reshape(b, l, SSD_GROUPS, D_STATE)
    cm = cm.reshape(b, l, SSD_GROUPS, D_STATE)
    dtr_f, dtr_b = jnp.split(dt_raw.astype(f32), 2, axis=-1)
    dt_f = jax.nn.softplus(dtr_f + dt_bias_f.astype(f32))
    dt_b = jax.nn.softplus(dtr_b + dt_bias_b.astype(f32))
    a_f = -jnp.exp(a_log_f.astype(f32))
    a_b = -jnp.exp(a_log_b.astype(f32))
    y_f, s_f = ssd_scan(xs, dt_f, a_f, bm, cm, h0_f)
    flip = lambda t: jnp.flip(t, axis=1)
    y_b, s_b = ssd_scan(flip(xs), flip(dt_b), a_b, flip(bm), flip(cm), h0_b)
    y = y_f + flip(y_b) + d_skip.astype(f32)[:, None] * xs.astype(f32)
    y = y.reshape(b, l, SSD_INNER) * jax.nn.silu(z.astype(f32))
    y = rms_norm(y.reshape(b, l, SSD_GROUPS, SSD_INNER // SSD_GROUPS),
                 norm_g.reshape(SSD_GROUPS, SSD_INNER // SSD_GROUPS)).reshape(b, l, SSD_INNER)
    return y.astype(h.dtype) @ w_out, s_f, s_b


def setup_inputs(seed: int = 0) -> dict:
    key = jax.random.key(seed)
    ks = jax.random.split(key, 32)
    f32 = jnp.float32
    nc, ns = N_CONV_LAYERS, N_SSD_LAYERS

    def nrm(k, shape, s):
        return jax.random.normal(k, shape, f32) * s

    def dt_bias(k):
        dt = jnp.exp(jax.random.uniform(k, (ns, SSD_HEADS), f32, math.log(1e-3), math.log(1e-1)))
        return dt + jnp.log(-jnp.expm1(-dt))

    state_shape = (DEC_BATCH, ns, SSD_HEADS, SSD_HEAD_DIM, D_STATE)
    return {
        'x_prompt': nrm(ks[0], (BATCH, SEQ, D_MODEL), 1.0),
        'x_sample': nrm(ks[1], (DEC_BATCH, DEC_SEQ, D_MODEL), 1.0),
        'state_fwd': nrm(ks[2], state_shape, 0.5),
        'state_bwd': nrm(ks[3], state_shape, 0.5),
        'c': nrm(ks[4], (DEC_BATCH, D_MODEL), 1.0),
        'c_ctx': nrm(ks[5], (D_MODEL,), 1.0),
        'ada_w': nrm(ks[6], (DEPTH, D_MODEL, 3 * D_MODEL), 0.5 * D_MODEL ** -0.5),
        'ada_b': nrm(ks[7], (DEPTH, 3 * D_MODEL), 0.02),
        'norm_g': 1.0 + nrm(ks[8], (DEPTH, D_MODEL), 0.1),
        'conv_w_in': nrm(ks[9], (nc, D_MODEL, 3 * CONV_WIDTH), D_MODEL ** -0.5),
        'conv_b_in': nrm(ks[10], (nc, 3 * CONV_WIDTH), 0.02),
        'conv_w_dw': nrm(ks[11], (nc, CONV_K, CONV_WIDTH), CONV_K ** -0.5),
        'conv_b_dw': nrm(ks[12], (nc, CONV_WIDTH), 0.02),
        'conv_ln_g': 1.0 + nrm(ks[13], (nc, CONV_WIDTH), 0.1),
        'conv_ln_b': nrm(ks[14], (nc, CONV_WIDTH), 0.02),
        'conv_w_out': nrm(ks[15], (nc, CONV_WIDTH, D_MODEL), CONV_WIDTH ** -0.5),
        'ssd_w_in': nrm(ks[16], (ns, D_MODEL, SSD_PROJ_DIM), D_MODEL ** -0.5),
        'ssd_w_conv': nrm(ks[17], (ns, SSD_CONV_K, XBC_DIM), SSD_CONV_K ** -0.5),
        'ssd_b_conv': nrm(ks[18], (ns, XBC_DIM), 0.02),
        'ssd_dt_bias_f': dt_bias(ks[19]),
        'ssd_dt_bias_b': dt_bias(ks[20]),
        'ssd_a_log_f': jnp.log(jax.random.uniform(ks[21], (ns, SSD_HEADS), f32, 1.0, 16.0)),
        'ssd_a_log_b': jnp.log(jax.random.uniform(ks[22], (ns, SSD_HEADS), f32, 1.0, 16.0)),
        'ssd_d': 1.0 + nrm(ks[23], (ns, SSD_HEADS), 0.1),
        'ssd_norm_g': 1.0 + nrm(ks[24], (ns, SSD_INNER), 0.1),
        'ssd_w_out': nrm(ks[25], (ns, SSD_INNER, D_MODEL), SSD_INNER ** -0.5),
        'final_norm_g': 1.0 + nrm(ks[26], (D_MODEL,), 0.1),
    }


def reference(x_prompt, x_sample, state_fwd, state_bwd, c, c_ctx, ada_w, ada_b, norm_g,
              conv_w_in, conv_b_in, conv_w_dw, conv_b_dw, conv_ln_g, conv_ln_b, conv_w_out,
              ssd_w_in, ssd_w_conv, ssd_b_conv, ssd_dt_bias_f, ssd_dt_bias_b, ssd_a_log_f, ssd_a_log_b,
              ssd_d, ssd_norm_g, ssd_w_out, final_norm_g):
    yp = x_prompt
    ys = x_sample
    cond_ctx = c_ctx[None, None, :]
    cond_lat = c[:, None, :]
    new_f = []
    new_b = []
    for i in range(DEPTH):
        j = i // N_MIXERS
        sh_p, sc_p, gt_p = adaln_params(cond_ctx, ada_w[i], ada_b[i])
        sh_s, sc_s, gt_s = adaln_params(cond_lat, ada_w[i], ada_b[i])
        hp = rms_norm(yp, norm_g[i]) * (1.0 + sc_p) + sh_p
        hs = rms_norm(ys, norm_g[i]) * (1.0 + sc_s) + sh_s
        if i % N_MIXERS == 0:
            cw = (conv_w_in[j], conv_b_in[j], conv_w_dw[j], conv_b_dw[j], conv_ln_g[j], conv_ln_b[j], conv_w_out[j])
            out_p = conformer_branch(hp, False, *cw)
            out_s = conformer_branch(hs, True, *cw)
        else:
            sw = (ssd_w_in[j], ssd_w_conv[j], ssd_b_conv[j], ssd_dt_bias_f[j], ssd_dt_bias_b[j],
                  ssd_a_log_f[j], ssd_a_log_b[j], ssd_d[j], ssd_norm_g[j], ssd_w_out[j])
            zero_state = jnp.zeros((yp.shape[0], SSD_HEADS, SSD_HEAD_DIM, D_STATE), jnp.float32)
            out_p, s_f, s_b = ssd_branch(hp, zero_state, zero_state, *sw)
            out_s, _, _ = ssd_branch(hs, state_fwd[:, j], state_bwd[:, j], *sw)
            new_f.append(s_f)
            new_b.append(s_b)
        yp = yp + gt_p * out_p
        ys = ys + gt_s * out_s
    y_prompt = rms_norm(yp, final_norm_g)
    y_sample = rms_norm(ys, final_norm_g)
    new_state_fwd = jnp.stack(new_f, axis=1).astype(x_prompt.dtype)
    new_state_bwd = jnp.stack(new_b, axis=1).astype(x_prompt.dtype)
    return (y_prompt, y_sample, new_state_fwd, new_state_bwd)
```

```python
import functools

import jax
import jax.numpy as jnp
from jax import lax
from jax.experimental import pallas as pl
from jax.experimental.pallas import tpu as pltpu

F32 = jnp.float32
BF16 = jnp.bfloat16

D_MODEL = 1024
CONV_E = 2048
CONV_K = 31
CONV_HALF = CONV_K // 2
SSD_INNER = 2048
SSD_HEADS = 32
SSD_HEAD_DIM = 64
SSD_GROUPS = 4
D_STATE = 128
SSD_CONV_K = 5
XBC = SSD_INNER + 2 * SSD_GROUPS * D_STATE
BC_W = SSD_GROUPS * D_STATE
CHUNK = 128
EPS = 1e-6

LANES = 128
SUBLANES = 8
N_SLABS = CONV_E // LANES
PAD = 16
ROW_CHUNK = 64
VMEM_LIMIT = 56 * 1024 * 1024


def _sigmoid(x):
    return jax.nn.sigmoid(x)


def _silu(x):
    return x * _sigmoid(x)


def _softplus(x):
    return jnp.maximum(x, 0.0) + jnp.log1p(jnp.exp(-jnp.abs(x)))


def _dot(a, b):
    return jnp.dot(a, b, preferred_element_type=F32)


def _adaln_kernel(cond_ref, w_ref, b_ref, o_ref):
    s = _silu(cond_ref[...])
    o_ref[0] = _dot(s.astype(BF16), w_ref[0].astype(BF16)) + b_ref[0]


def _adaln(cond16, ada_w, ada_b):
    depth = ada_w.shape[0]
    nt = 4
    tn = 3 * D_MODEL // nt
    return pl.pallas_call(
        _adaln_kernel,
        grid=(depth, nt),
        in_specs=[
            pl.BlockSpec((16, D_MODEL), lambda i, j: (0, 0)),
            pl.BlockSpec((1, D_MODEL, tn), lambda i, j: (i, 0, j)),
            pl.BlockSpec((1, 1, tn), lambda i, j: (i, 0, j)),
        ],
        out_specs=pl.BlockSpec((1, 16, tn), lambda i, j: (i, 0, j)),
        out_shape=jax.ShapeDtypeStruct((depth, 16, 3 * D_MODEL), F32),
        name="adaln",
    )(cond16, ada_w, ada_b.reshape(depth, 1, 3 * D_MODEL))


def _modulated_norm_to(h_ref, x_ref, g_ref, mod_ref, tm):
    g = g_ref[...]
    sh = mod_ref[0, 0, 0:1, :]
    sc = mod_ref[0, 0, 1:2, :]

    def body(r, carry):
        rows = pl.ds(pl.multiple_of(r * 32, 32), 32)
        x = x_ref[rows, :]
        ms = jnp.mean(x * x, axis=-1, keepdims=True)
        xn = x * lax.rsqrt(ms + EPS) * g
        h_ref[rows, :] = (xn * (1.0 + sc) + sh).astype(BF16)
        return carry

    lax.fori_loop(0, tm // 32, body, 0)


def _conv_geometry(seg_len, tm):
    n_seg = tm // seg_len
    if seg_len == 64:
        assert n_seg == SUBLANES
        pitch, stride, n_grp, grp_pitch, n_t0 = 84, 84, 1, 0, seg_len
    else:
        assert seg_len == 256
        pitch, stride, n_grp, grp_pitch, n_t0 = 304, 36, n_seg, 304, 36
    rows = PAD + n_seg * pitch
    rows = -(-(rows + PAD) // SUBLANES) * SUBLANES if seg_len == 64 else rows
    pieces = []
    for p in range(tm // ROW_CHUNK):
        t = p * ROW_CHUNK
        pieces.append((t, PAD + pitch * (t // seg_len) + (t % seg_len)))
    return dict(pitch=pitch, stride=stride, n_grp=n_grp, grp_pitch=grp_pitch, n_t0=n_t0, rows=rows,
                pieces=tuple(pieces))


def _conformer_kernel(x_ref, mod_ref, g_ref, win_ref, bin_ref, wdw_ref, bdw_ref, lng_ref, lnb_ref, wout_ref,
                      o_ref, h_ref, cin_ref, cout_ref, zs_ref, m_ref, *, tm, geo, cw, mb):
    nch = CONV_E // cw
    stride = geo["stride"]

    @pl.when(pl.program_id(0) == 0)
    def _():
        cin_ref[...] = jnp.zeros(cin_ref.shape, F32)

    _modulated_norm_to(h_ref, x_ref, g_ref, mod_ref, tm)

    def proj_body(n, carry):
        for m in range(tm // mb):
            hm = h_ref[m * mb:(m + 1) * mb, :]
            a = _dot(hm, win_ref[0, n]) + bin_ref[0, n]
            gt = _dot(hm, win_ref[1, n]) + bin_ref[1, n]
            v = a * _sigmoid(gt)
            for q in range(cw // LANES):
                for (t, crow) in geo["pieces"]:
                    if m * mb <= t < (m + 1) * mb:
                        cin_ref[n * (cw // LANES) + q, pl.ds(crow, ROW_CHUNK), :] = (
                            v[t - m * mb:t - m * mb + ROW_CHUNK, q * LANES:(q + 1) * LANES])
            z = _dot(hm, win_ref[2, n]) + bin_ref[2, n]
            zs_ref[n, m * mb:(m + 1) * mb, :] = _silu(z)
        return carry

    lax.fori_loop(0, nch, proj_body, 0)

    tb = 4

    def slab_body(sl, carry):
        wk = [jnp.broadcast_to(wdw_ref[sl, k:k + 1, :], (SUBLANES, LANES)) for k in range(CONV_K)]
        bias = jnp.broadcast_to(bdw_ref[sl], (SUBLANES, LANES))
        for grp in range(geo["n_grp"]):
            def t_body(ti, c2):
                for u in range(tb):
                    base = PAD + grp * geo["grp_pitch"] + ti * tb + u
                    acc = bias
                    for k in range(CONV_K):
                        acc = acc + wk[k] * cin_ref[sl, pl.ds(base + (k - CONV_HALF), SUBLANES, stride=stride), :]
                    cout_ref[sl, pl.ds(base, SUBLANES, stride=stride), :] = acc
                return c2

            lax.fori_loop(0, geo["n_t0"] // tb, t_body, 0)
        return carry

    lax.fori_loop(0, N_SLABS, slab_body, 0)

    for (t, crow) in geo["pieces"]:
        rows = pl.ds(crow, ROW_CHUNK)
        s1 = cout_ref[0, rows, :]
        for sl in range(1, N_SLABS):
            s1 = s1 + cout_ref[sl, rows, :]
        mu = jnp.sum(s1, axis=-1, keepdims=True) * (1.0 / CONV_E)
        s2 = jnp.zeros((ROW_CHUNK, LANES), F32)
        for sl in range(N_SLABS):
            d = cout_ref[sl, rows, :] - mu
            s2 = s2 + d * d
        var = jnp.sum(s2, axis=-1, keepdims=True) * (1.0 / CONV_E)
        rstd = lax.rsqrt(var + EPS)
        for sl in range(N_SLABS):
            cs = slice(sl * LANES, (sl + 1) * LANES)
            yn = (cout_ref[sl, rows, :] - mu) * rstd * lng_ref[:, cs] + lnb_ref[:, cs]
            zq = zs_ref[sl * LANES // cw, t:t + ROW_CHUNK, (sl * LANES) % cw:(sl * LANES) % cw + LANES]
            m_ref[t:t + ROW_CHUNK, cs] = (_silu(yn) * zq).astype(BF16)

    gate = mod_ref[0, 0, 2:3, :]
    for m in range(tm // mb):
        rows = slice(m * mb, (m + 1) * mb)
        out = _dot(m_ref[rows, :], wout_ref[...])
        o_ref[rows, :] = x_ref[rows, :] + gate * out


def _conformer_layer(x2d, mods4, layer, cond_of_block, seg_len, g_row, win3, bin3, wdw_s, bdw_s, lng, lnb, wout):
    n_tok = x2d.shape[0]
    tm = 512
    cw = 256
    mb = 256
    geo = _conv_geometry(seg_len, tm)
    nch = CONV_E // cw
    const = dict(pipeline_mode=pl.Buffered(1))
    kern = functools.partial(_conformer_kernel, tm=tm, geo=geo, cw=cw, mb=mb)
    return pl.pallas_call(
        kern,
        grid=(n_tok // tm,),
        in_specs=[
            pl.BlockSpec((tm, D_MODEL), lambda i: (i, 0)),
            pl.BlockSpec((1, 1, 3, D_MODEL), lambda i: (layer, cond_of_block(i), 0, 0)),
            pl.BlockSpec((1, D_MODEL), lambda i: (0, 0)),
            pl.BlockSpec((3, nch, D_MODEL, cw), lambda i: (0, 0, 0, 0), **const),
            pl.BlockSpec((3, nch, 1, cw), lambda i: (0, 0, 0, 0)),
            pl.BlockSpec((N_SLABS, 32, LANES), lambda i: (0, 0, 0)),
            pl.BlockSpec((N_SLABS, 1, LANES), lambda i: (0, 0, 0)),
            pl.BlockSpec((1, CONV_E), lambda i: (0, 0)),
            pl.BlockSpec((1, CONV_E), lambda i: (0, 0)),
            pl.BlockSpec((CONV_E, D_MODEL), lambda i: (0, 0), **const),
        ],
        out_specs=pl.BlockSpec((tm, D_MODEL), lambda i: (i, 0)),
        out_shape=jax.ShapeDtypeStruct((n_tok, D_MODEL), F32),
        scratch_shapes=[
            pltpu.VMEM((tm, D_MODEL), BF16),
            pltpu.VMEM((N_SLABS, geo["rows"], LANES), F32),
            pltpu.VMEM((N_SLABS, geo["rows"], LANES), F32),
            pltpu.VMEM((nch, tm, cw), F32),
            pltpu.VMEM((tm, CONV_E), BF16),
        ],
        compiler_params=pltpu.CompilerParams(dimension_semantics=("arbitrary",), vmem_limit_bytes=VMEM_LIMIT),
        name="conformer_seg%d" % seg_len,
    )(x2d, mods4, g_row, win3, bin3, wdw_s, bdw_s, lng, lnb, wout)


def _ssd_proj_kernel(x_ref, mod_ref, g_ref, wz_ref, wx_ref, wd_ref, z_ref, xbc_ref, dt_ref, h_ref, *, tm, mb, nw):
    _modulated_norm_to(h_ref, x_ref, g_ref, mod_ref, tm)
    for m in range(tm // mb):
        rows = slice(m * mb, (m + 1) * mb)
        hm = h_ref[rows, :]
        for j in range(SSD_INNER // nw):
            cs = slice(j * nw, (j + 1) * nw)
            z_ref[rows, cs] = _dot(hm, wz_ref[:, cs])
        for j in range(XBC // nw):
            cs = slice(j * nw, (j + 1) * nw)
            xbc_ref[rows, cs] = _dot(hm, wx_ref[:, cs])
        dt_ref[rows, :] = _dot(hm, wd_ref[...])


def _ssd_proj(x2d, mods4, layer, cond_of_block, g_row, wz, wx, wd):
    n_tok = x2d.shape[0]
    tm = 512
    const = dict(pipeline_mode=pl.Buffered(1))
    kern = functools.partial(_ssd_proj_kernel, tm=tm, mb=256, nw=512)
    return pl.pallas_call(
        kern,
        grid=(n_tok // tm,),
        in_specs=[
            pl.BlockSpec((tm, D_MODEL), lambda i: (i, 0)),
            pl.BlockSpec((1, 1, 3, D_MODEL), lambda i: (layer, cond_of_block(i), 0, 0)),
            pl.BlockSpec((1, D_MODEL), lambda i: (0, 0)),
            pl.BlockSpec((D_MODEL, SSD_INNER), lambda i: (0, 0), **const),
            pl.BlockSpec((D_MODEL, XBC), lambda i: (0, 0), **const),
            pl.BlockSpec((D_MODEL, LANES), lambda i: (0, 0), **const),
        ],
        out_specs=[
            pl.BlockSpec((tm, SSD_INNER), lambda i: (i, 0)),
            pl.BlockSpec((tm, XBC), lambda i: (i, 0)),
            pl.BlockSpec((tm, LANES), lambda i: (i, 0)),
        ],
        out_shape=[
            jax.ShapeDtypeStruct((n_tok, SSD_INNER), F32),
            jax.ShapeDtypeStruct((n_tok, XBC), F32),
            jax.ShapeDtypeStruct((n_tok, LANES), F32),
        ],
        scratch_shapes=[pltpu.VMEM((tm, D_MODEL), BF16)],
        compiler_params=pltpu.CompilerParams(dimension_semantics=("arbitrary",), vmem_limit_bytes=VMEM_LIMIT),
        name="ssd_proj",
    )(x2d, mods4, g_row, wz, wx, wd)


def _split3(x):
    hi = x.astype(BF16)
    r1 = x - hi.astype(F32)
    mid = r1.astype(BF16)
    lo = (r1 - mid.astype(F32)).astype(BF16)
    return hi, mid, lo


def _ssd_scan_kernel(*refs, n_chunks, has_h0, emit_state):
    it = iter(refs)
    cur_ref, prev_ref, next_ref, dt_ref, z_ref, x_ref, mod_ref, fing_ref = (next(it) for _ in range(8))
    wconv_ref, bconv_ref, dtb_ref, arow_ref, dexp_ref, ng_ref, wout_ref = (next(it) for _ in range(7))
    if has_h0:
        h0f_ref, h0b_ref = next(it), next(it)
    o_ref = next(it)
    if emit_state:
        sf_ref, sb_ref = next(it), next(it)
    ext_ref, act_ref, yb_ref, ht_ref, gt_ref, xw_ref, y_ref, m_ref = (next(it) for _ in range(8))

    t = CHUNK
    s = pl.program_id(1)
    c = pl.program_id(2)
    ci = s * c + (1 - s) * (n_chunks - 1 - c)

    ext_ref[0:8, :] = jnp.where(ci > 0, prev_ref[...], 0.0)
    ext_ref[8:8 + t, :] = cur_ref[...]
    ext_ref[8 + t:16 + t, :] = jnp.where(ci < n_chunks - 1, next_ref[...], 0.0)
    half = SSD_CONV_K // 2
    for j in range(XBC // 512):
        cs = slice(j * 512, (j + 1) * 512)
        acc = jnp.broadcast_to(bconv_ref[:, cs], (t, 512))
        for k in range(SSD_CONV_K):
            acc = acc + wconv_ref[k:k + 1, cs] * ext_ref[pl.ds(8 - half + k, t), cs]
        act_ref[:, cs] = _silu(acc)

    dt_all = _softplus(dt_ref[...] + dtb_ref[...])
    la = dt_all * arow_ref[...]
    row = lax.broadcasted_iota(jnp.int32, (t, t), 0)
    col = lax.broadcasted_iota(jnp.int32, (t, t), 1)
    lower = col <= row
    upper = col >= row
    lmat = jnp.where(lower, 1.0, 0.0).astype(BF16)
    umat = jnp.where(upper, 1.0, 0.0).astype(BF16)
    parts = _split3(la)
    cum_f = _dot(lmat, parts[0]) + _dot(lmat, parts[1]) + _dot(lmat, parts[2])
    cum_b = _dot(umat, parts[0]) + _dot(umat, parts[1]) + _dot(umat, parts[2])
    cum = jnp.where(col < SSD_HEADS, cum_f, cum_b)
    cum_t = cum.T
    lane_lo = col < SSD_HEAD_DIM

    def colb(a, j):
        return jnp.broadcast_to(a[:, j:j + 1], (t, LANES))

    def pair(a, j0):
        return jnp.where(lane_lo, colb(a, j0), colb(a, j0 + 1))

    def state_update(st_ref, g, decay_row):
        gs = slice(g * 512, (g + 1) * 512)
        b_t = act_ref[:, SSD_INNER + g * D_STATE:SSD_INNER + (g + 1) * D_STATE].T.astype(BF16)
        st_ref[:, gs] = st_ref[:, gs] * decay_row + _dot(b_t, xw_ref[:, gs])

    hp_per_g = SSD_HEADS // SSD_GROUPS // 2

    @pl.when(s == 0)
    def _():
        @pl.when(c == 0)
        def _():
            if has_h0:
                gt_ref[...] = h0b_ref[0].T
            else:
                gt_ref[...] = jnp.zeros(gt_ref.shape, F32)

        for g in range(SSD_GROUPS):
            gs = slice(g * 512, (g + 1) * 512)
            c_g = act_ref[:, SSD_INNER + BC_W + g * D_STATE:SSD_INNER + BC_W + (g + 1) * D_STATE].astype(BF16)
            yoff = _dot(c_g, gt_ref[:, gs].astype(BF16))
            decays = []
            for q in range(hp_per_g):
                hp = g * hp_per_g + q
                ps = slice(hp * LANES, (hp + 1) * LANES)
                r_pair = pair(cum, SSD_HEADS + 2 * hp)
                dt_pair = pair(dt_all, SSD_HEADS + 2 * hp)
                edge = r_pair[0:1, :]
                yb_ref[ci, :, ps] = yoff[:, q * LANES:(q + 1) * LANES] * jnp.exp(r_pair)
                w1 = dt_pair * jnp.exp(edge - r_pair)
                xw_ref[:, ps] = (act_ref[:, ps] * w1).astype(BF16)
                decays.append(jnp.exp(edge))
            state_update(gt_ref, g, jnp.concatenate(decays, axis=1))

        if emit_state:
            @pl.when(c == n_chunks - 1)
            def _():
                sb_ref[0] = gt_ref[...].T

    @pl.when(s == 1)
    def _():
        @pl.when(c == 0)
        def _():
            if has_h0:
                ht_ref[...] = h0f_ref[0].T
            else:
                ht_ref[...] = jnp.zeros(ht_ref.shape, F32)

        for g in range(SSD_GROUPS):
            gs = slice(g * 512, (g + 1) * 512)
            b_g = act_ref[:, SSD_INNER + g * D_STATE:SSD_INNER + (g + 1) * D_STATE].astype(BF16)
            c_g = act_ref[:, SSD_INNER + BC_W + g * D_STATE:SSD_INNER + BC_W + (g + 1) * D_STATE].astype(BF16)
            cb = lax.dot_general(c_g, b_g, (((1,), (1,)), ((), ())), preferred_element_type=F32)
            yoff = _dot(c_g, ht_ref[:, gs].astype(BF16))
            decays = []
            for q in range(hp_per_g):
                hp = g * hp_per_g + q
                ps = slice(hp * LANES, (hp + 1) * LANES)
                lhs = []
                for e in range(2):
                    h = 2 * hp + e
                    seg_f = colb(cum, h) - jnp.broadcast_to(cum_t[h:h + 1, :], (t, t))
                    seg_b = colb(cum, SSD_HEADS + h) - jnp.broadcast_to(cum_t[SSD_HEADS + h:SSD_HEADS + h + 1, :], (t, t))
                    lhs.append((cb * jnp.where(lower, jnp.exp(seg_f), 0.0)).astype(BF16))
                    lhs.append((cb * jnp.where(upper, jnp.exp(seg_b), 0.0)).astype(BF16))
                xs_pair = act_ref[:, ps]
                dtf_pair = pair(dt_all, 2 * hp)
                dtb_pair = pair(dt_all, SSD_HEADS + 2 * hp)
                xdt_f = xs_pair * dtf_pair
                xdt_b = xs_pair * dtb_pair
                rhs = [jnp.where(lane_lo, xdt_f, 0.0).astype(BF16), jnp.where(lane_lo, xdt_b, 0.0).astype(BF16),
                       jnp.where(lane_lo, 0.0, xdt_f).astype(BF16), jnp.where(lane_lo, 0.0, xdt_b).astype(BF16)]
                y_diag = _dot(jnp.concatenate(lhs, axis=1), jnp.concatenate(rhs, axis=0))
                cf_pair = pair(cum, 2 * hp)
                edge = cf_pair[t - 1:t, :]
                y_ref[:, ps] = (y_diag + yoff[:, q * LANES:(q + 1) * LANES] * jnp.exp(cf_pair)
                                + yb_ref[ci, :, ps] + dexp_ref[:, ps] * xs_pair)
                w1 = dtf_pair * jnp.exp(edge - cf_pair)
                xw_ref[:, ps] = (xs_pair * w1).astype(BF16)
                decays.append(jnp.exp(edge))
            state_update(ht_ref, g, jnp.concatenate(decays, axis=1))

        if emit_state:
            @pl.when(c == n_chunks - 1)
            def _():
                sf_ref[0] = ht_ref[...].T

        gw = SSD_INNER // SSD_GROUPS
        for k in range(SSD_GROUPS):
            ks = slice(k * gw, (k + 1) * gw)
            yg = y_ref[:, ks] * _silu(z_ref[:, ks])
            ms = jnp.mean(yg * yg, axis=-1, keepdims=True)
            m_ref[:, ks] = (yg * lax.rsqrt(ms + EPS) * ng_ref[:, ks]).astype(BF16)
        out = _dot(m_ref[...], wout_ref[...])
        ynew = x_ref[...] + mod_ref[0, 0, 2:3, :] * out
        ms = jnp.mean(ynew * ynew, axis=-1, keepdims=True)
        o_ref[...] = ynew * lax.rsqrt(ms + EPS) * fing_ref[...]


def _ssd_scan(x2d, z, xbc, dtr, mods4, layer, cond_of_batch, seq_len, fin_g, wconv, bconv, dtb_row, a_row, d_exp,
              ng_row, wout, h0f, h0b, emit_state):
    n_tok = x2d.shape[0]
    n_batch = n_tok // seq_len
    nc = seq_len // CHUNK
    has_h0 = h0f is not None
    hb = CHUNK // SUBLANES
    last_hb = n_tok // SUBLANES - 1

    def ci_of(s, c):
        return s * c + (1 - s) * (nc - 1 - c)

    def blk(b, s, c):
        return b * nc + ci_of(s, c)

    const = dict(pipeline_mode=pl.Buffered(1))
    hp = SSD_HEADS * SSD_HEAD_DIM
    in_specs = [
        pl.BlockSpec((CHUNK, XBC), lambda b, s, c: (blk(b, s, c), 0)),
        pl.BlockSpec((SUBLANES, XBC), lambda b, s, c: (jnp.maximum(blk(b, s, c) * hb - 1, 0), 0)),
        pl.BlockSpec((SUBLANES, XBC), lambda b, s, c: (jnp.minimum((blk(b, s, c) + 1) * hb, last_hb), 0)),
        pl.BlockSpec((CHUNK, LANES), lambda b, s, c: (blk(b, s, c), 0)),
        pl.BlockSpec((CHUNK, SSD_INNER), lambda b, s, c: (b * nc + s * c, 0)),
        pl.BlockSpec((CHUNK, D_MODEL), lambda b, s, c: (b * nc + s * c, 0)),
        pl.BlockSpec((1, 1, 3, D_MODEL), lambda b, s, c: (layer, cond_of_batch(b), 0, 0)),
        pl.BlockSpec((1, D_MODEL), lambda b, s, c: (0, 0)),
        pl.BlockSpec((SUBLANES, XBC), lambda b, s, c: (0, 0)),
        pl.BlockSpec((1, XBC), lambda b, s, c: (0, 0)),
        pl.BlockSpec((1, LANES), lambda b, s, c: (0, 0)),
        pl.BlockSpec((1, LANES), lambda b, s, c: (0, 0)),
        pl.BlockSpec((1, SSD_INNER), lambda b, s, c: (0, 0)),
        pl.BlockSpec((1, SSD_INNER), lambda b, s, c: (0, 0)),
        pl.BlockSpec((SSD_INNER, D_MODEL), lambda b, s, c: (0, 0), **const),
    ]
    args = [xbc, xbc, xbc, dtr, z, x2d, mods4, fin_g, wconv, bconv, dtb_row, a_row, d_exp, ng_row, wout]
    if has_h0:
        in_specs += [pl.BlockSpec((1, hp, D_STATE), lambda b, s, c: (b, 0, 0))] * 2
        args += [h0f, h0b]
    out_specs = [pl.BlockSpec((CHUNK, D_MODEL), lambda b, s, c: (b * nc + s * c, 0))]
    out_shape = [jax.ShapeDtypeStruct((n_tok, D_MODEL), F32)]
    if emit_state:
        out_specs += [pl.BlockSpec((1, hp, D_STATE), lambda b, s, c: (b, 0, 0))] * 2
        out_shape += [jax.ShapeDtypeStruct((n_batch, hp, D_STATE), F32)] * 2
    kern = functools.partial(_ssd_scan_kernel, n_chunks=nc, has_h0=has_h0, emit_state=emit_state)
    return pl.pallas_call(
        kern,
        grid=(n_batch, 2, nc),
        in_specs=in_specs,
        out_specs=out_specs,
        out_shape=out_shape,
        scratch_shapes=[
            pltpu.VMEM((CHUNK + 2 * SUBLANES, XBC), F32),
            pltpu.VMEM((CHUNK, XBC), F32),
            pltpu.VMEM((nc, CHUNK, SSD_INNER), F32),
            pltpu.VMEM((D_STATE, hp), F32),
            pltpu.VMEM((D_STATE, hp), F32),
            pltpu.VMEM((CHUNK, SSD_INNER), BF16),
            pltpu.VMEM((CHUNK, SSD_INNER), F32),
            pltpu.VMEM((CHUNK, SSD_INNER), BF16),
        ],
        compiler_params=pltpu.CompilerParams(dimension_semantics=("arbitrary", "arbitrary", "arbitrary"),
                                             vmem_limit_bytes=VMEM_LIMIT),
        name="ssd_scan_L%d" % seq_len,
    )(*args)


def kernel(x_prompt, x_sample, state_fwd, state_bwd, c, c_ctx, ada_w, ada_b, norm_g, conv_w_in, conv_b_in, conv_w_dw, conv_b_dw, conv_ln_g, conv_ln_b, conv_w_out, ssd_w_in, ssd_w_conv, ssd_b_conv, ssd_dt_bias_f, ssd_dt_bias_b, ssd_a_log_f, ssd_a_log_b, ssd_d, ssd_norm_g, ssd_w_out, final_norm_g):
    nb_p, seq_p, _ = x_prompt.shape
    nb_s, seq_s, _ = x_sample.shape
    grid_w = 64

    cond = jnp.concatenate([c_ctx[None, :], c, jnp.zeros((16 - 1 - nb_s, D_MODEL), F32)], axis=0)
    mods4 = _adaln(cond, ada_w, ada_b).reshape(ada_w.shape[0], 16, 3, D_MODEL)

    xp = x_prompt.reshape(nb_p * seq_p, D_MODEL)
    xs = x_sample.reshape(nb_s * seq_s, D_MODEL)

    cw = 256
    nch = CONV_E // cw
    win3 = conv_w_in[0].reshape(D_MODEL, 3, nch, cw).transpose(1, 2, 0, 3).astype(BF16)
    bin3 = conv_b_in[0].reshape(3, nch, 1, cw)
    wdw_s = jnp.pad(conv_w_dw[0], ((0, 32 - CONV_K), (0, 0))).reshape(32, N_SLABS, LANES).transpose(1, 0, 2)
    bdw_s = conv_b_dw[0].reshape(N_SLABS, 1, LANES)
    lng = conv_ln_g[0].reshape(1, CONV_E)
    lnb = conv_ln_b[0].reshape(1, CONV_E)
    wout0 = conv_w_out[0].astype(BF16)
    g0 = norm_g[0].reshape(1, D_MODEL)
    tm = 512
    xp = _conformer_layer(xp, mods4, 0, lambda i: 0, seq_p, g0, win3, bin3, wdw_s, bdw_s, lng, lnb, wout0)
    xs = _conformer_layer(xs, mods4, 0, lambda i: 1 + i // (seq_s // tm), grid_w, g0, win3, bin3, wdw_s, bdw_s,
                          lng, lnb, wout0)

    w_in = ssd_w_in[0]
    wz = w_in[:, :SSD_INNER].astype(BF16)
    wx = w_in[:, SSD_INNER:SSD_INNER + XBC].astype(BF16)
    wd = jnp.pad(w_in[:, SSD_INNER + XBC:], ((0, 0), (0, LANES - 2 * SSD_HEADS))).astype(BF16)
    g1 = norm_g[1].reshape(1, D_MODEL)
    wconv = jnp.pad(ssd_w_conv[0], ((0, SUBLANES - SSD_CONV_K), (0, 0)))
    bconv = ssd_b_conv[0].reshape(1, XBC)
    zpad = jnp.zeros((LANES - 2 * SSD_HEADS,), F32)
    dtb_row = jnp.concatenate([ssd_dt_bias_f[0], ssd_dt_bias_b[0], zpad]).reshape(1, LANES)
    a_row = jnp.concatenate([-jnp.exp(ssd_a_log_f[0]), -jnp.exp(ssd_a_log_b[0]), zpad]).reshape(1, LANES)
    d_exp = jnp.repeat(ssd_d[0], SSD_HEAD_DIM).reshape(1, SSD_INNER)
    ng_row = ssd_norm_g[0].reshape(1, SSD_INNER)
    wout1 = ssd_w_out[0].astype(BF16)
    fin_g = final_norm_g.reshape(1, D_MODEL)
    hp = SSD_HEADS * SSD_HEAD_DIM

    zp, xbcp, dtp = _ssd_proj(xp, mods4, 1, lambda i: 0, g1, wz, wx, wd)
    zs_, xbcs, dts = _ssd_proj(xs, mods4, 1, lambda i: 1 + i // (seq_s // tm), g1, wz, wx, wd)

    yp, sf, sb = _ssd_scan(xp, zp, xbcp, dtp, mods4, 1, lambda b: 0, seq_p, fin_g, wconv, bconv, dtb_row, a_row,
                           d_exp, ng_row, wout1, None, None, True)
    (ys,) = _ssd_scan(xs, zs_, xbcs, dts, mods4, 1, lambda b: 1 + b, seq_s, fin_g, wconv, bconv, dtb_row, a_row,
                      d_exp, ng_row, wout1, state_fwd[:, 0].reshape(nb_s, hp, D_STATE),
                      state_bwd[:, 0].reshape(nb_s, hp, D_STATE), False)

    y_prompt = yp.reshape(nb_p, seq_p, D_MODEL)
    y_sample = ys.reshape(nb_s, seq_s, D_MODEL)
    new_f = sf.reshape(nb_p, 1, SSD_HEADS, SSD_HEAD_DIM, D_STATE)
    new_b = sb.reshape(nb_p, 1, SSD_HEADS, SSD_HEAD_DIM, D_STATE)
    return (y_prompt, y_sample, new_f, new_b)
```

```python
import functools

import jax
import jax.numpy as jnp
from jax import lax
from jax.experimental import pallas as pl
from jax.experimental.pallas import tpu as pltpu

F32 = jnp.float32
BF16 = jnp.bfloat16

D_MODEL = 1024
CONV_E = 2048
CONV_K = 31
CONV_HALF = CONV_K // 2
SSD_INNER = 2048
SSD_HEADS = 32
SSD_HEAD_DIM = 64
SSD_GROUPS = 4
D_STATE = 128
SSD_CONV_K = 5
SSD_CONV_HALF = SSD_CONV_K // 2
XBC = SSD_INNER + 2 * SSD_GROUPS * D_STATE
CHUNK = 128
EPS = 1e-6

LANES = 128
SUBLANES = 8
N_SLABS = CONV_E // LANES
XBC_SLABS = XBC // LANES
X_SLABS = SSD_INNER // LANES
B_SLAB0 = X_SLABS
C_SLAB0 = X_SLABS + SSD_GROUPS
PAD = 16
HALO = 16
ROW_CHUNK = 64
VMEM_LIMIT = 56 * 1024 * 1024


def _sigmoid(x):
    return jax.nn.sigmoid(x)


def _silu(x):
    return x * _sigmoid(x)


def _softplus(x):
    return jnp.maximum(x, 0.0) + jnp.log1p(jnp.exp(-jnp.abs(x)))


def _dot(a, b):
    return jnp.dot(a, b, preferred_element_type=F32)


def _adaln_kernel(cond_ref, w_ref, b_ref, o_ref):
    s = _silu(cond_ref[...])
    o_ref[0] = _dot(s.astype(BF16), w_ref[0].astype(BF16)) + b_ref[0]


def _adaln(cond16, ada_w, ada_b):
    depth = ada_w.shape[0]
    nt = 4
    tn = 3 * D_MODEL // nt
    return pl.pallas_call(
        _adaln_kernel,
        grid=(depth, nt),
        in_specs=[
            pl.BlockSpec((16, D_MODEL), lambda i, j: (0, 0)),
            pl.BlockSpec((1, D_MODEL, tn), lambda i, j: (i, 0, j)),
            pl.BlockSpec((1, 1, tn), lambda i, j: (i, 0, j)),
        ],
        out_specs=pl.BlockSpec((1, 16, tn), lambda i, j: (i, 0, j)),
        out_shape=jax.ShapeDtypeStruct((depth, 16, 3 * D_MODEL), F32),
        name="adaln",
    )(cond16, ada_w, ada_b.reshape(depth, 1, 3 * D_MODEL))


def _modulated_norm(x, g, sh, sc):
    ms = jnp.mean(x * x, axis=-1, keepdims=True)
    return (x * lax.rsqrt(ms + EPS) * g) * (1.0 + sc) + sh


def _modulated_norm_to(h_ref, x_ref, g_ref, mod_ref, tm):
    g = g_ref[...]
    sh = mod_ref[0, 0, 0:1, :]
    sc = mod_ref[0, 0, 1:2, :]

    def body(r, carry):
        rows = pl.ds(pl.multiple_of(r * 32, 32), 32)
        h_ref[rows, :] = _modulated_norm(x_ref[rows, :], g, sh, sc).astype(BF16)
        return carry

    lax.fori_loop(0, tm // 32, body, 0)


def _conv_geometry(seg_len, tm):
    n_seg = tm // seg_len
    if seg_len == 64:
        assert n_seg == SUBLANES
        pitch, stride, n_grp, grp_pitch, n_t0, t_unroll = 84, 84, 1, 0, seg_len, 16
    else:
        assert seg_len == 256
        pitch, stride, n_grp, grp_pitch, n_t0, t_unroll = 304, 36, n_seg, 304, 36, 12
    rows = PAD + n_seg * pitch
    rows = -(-(rows + PAD) // SUBLANES) * SUBLANES if seg_len == 64 else rows
    pieces = []
    for p in range(tm // ROW_CHUNK):
        t = p * ROW_CHUNK
        pieces.append((t, PAD + pitch * (t // seg_len) + (t % seg_len)))
    assert n_t0 % t_unroll == 0
    return dict(pitch=pitch, stride=stride, n_grp=n_grp, grp_pitch=grp_pitch, n_t0=n_t0, rows=rows,
                t_unroll=t_unroll, pieces=tuple(pieces))


def _conformer_kernel(x_ref, mod_ref, g_ref, win_ref, bin_ref, wdw_ref, bdw_ref, lng_ref, lnb_ref, wout_ref,
                      o_ref, h_ref, cin_ref, cout_ref, zs_ref, m_ref, *, tm, geo, cw, mb):
    nch = CONV_E // cw
    stride = geo["stride"]

    @pl.when(pl.program_id(0) == 0)
    def _():
        cin_ref[...] = jnp.zeros(cin_ref.shape, F32)

    _modulated_norm_to(h_ref, x_ref, g_ref, mod_ref, tm)

    def proj_body(n, carry):
        for m in range(tm // mb):
            hm = h_ref[m * mb:(m + 1) * mb, :]
            a = _dot(hm, win_ref[0, n]) + bin_ref[0, n]
            gt = _dot(hm, win_ref[1, n]) + bin_ref[1, n]
            v = a * _sigmoid(gt)
            for q in range(cw // LANES):
                for (t, crow) in geo["pieces"]:
                    if m * mb <= t < (m + 1) * mb:
                        cin_ref[n * (cw // LANES) + q, pl.ds(crow, ROW_CHUNK), :] = (
                            v[t - m * mb:t - m * mb + ROW_CHUNK, q * LANES:(q + 1) * LANES])
            z = _dot(hm, win_ref[2, n]) + bin_ref[2, n]
            zs_ref[n, m * mb:(m + 1) * mb, :] = _silu(z)
        return carry

    lax.fori_loop(0, nch, proj_body, 0)

    tb = geo["t_unroll"]

    def slab_body(sl, carry):
        wk = [jnp.broadcast_to(wdw_ref[sl, k:k + 1, :], (SUBLANES, LANES)) for k in range(CONV_K)]
        bias = jnp.broadcast_to(bdw_ref[sl], (SUBLANES, LANES))
        for grp in range(geo["n_grp"]):
            def t_body(ti, c2):
                base = PAD + grp * geo["grp_pitch"] + ti * tb
                accs = [bias] * tb
                for j in range(tb + CONV_K - 1):
                    xrow = cin_ref[sl, pl.ds(base + (j - CONV_HALF), SUBLANES, stride=stride), :]
                    for u in range(max(0, j - CONV_K + 1), min(tb, j + 1)):
                        accs[u] = accs[u] + wk[j - u] * xrow
                for u in range(tb):
                    cout_ref[sl, pl.ds(base + u, SUBLANES, stride=stride), :] = accs[u]
                return c2

            lax.fori_loop(0, geo["n_t0"] // tb, t_body, 0)
        return carry

    lax.fori_loop(0, N_SLABS, slab_body, 0)

    for (t, crow) in geo["pieces"]:
        rows = pl.ds(crow, ROW_CHUNK)
        s1 = cout_ref[0, rows, :]
        for sl in range(1, N_SLABS):
            s1 = s1 + cout_ref[sl, rows, :]
        mu = jnp.sum(s1, axis=-1, keepdims=True) * (1.0 / CONV_E)
        s2 = jnp.zeros((ROW_CHUNK, LANES), F32)
        for sl in range(N_SLABS):
            d = cout_ref[sl, rows, :] - mu
            s2 = s2 + d * d
        var = jnp.sum(s2, axis=-1, keepdims=True) * (1.0 / CONV_E)
        rstd = lax.rsqrt(var + EPS)
        for sl in range(N_SLABS):
            cs = slice(sl * LANES, (sl + 1) * LANES)
            yn = (cout_ref[sl, rows, :] - mu) * rstd * lng_ref[:, cs] + lnb_ref[:, cs]
            zq = zs_ref[sl * LANES // cw, t:t + ROW_CHUNK, (sl * LANES) % cw:(sl * LANES) % cw + LANES]
            m_ref[t:t + ROW_CHUNK, cs] = (_silu(yn) * zq).astype(BF16)

    gate = mod_ref[0, 0, 2:3, :]
    for m in range(tm // mb):
        rows = slice(m * mb, (m + 1) * mb)
        out = _dot(m_ref[rows, :], wout_ref[...])
        o_ref[rows, :] = x_ref[rows, :] + gate * out


def _conformer_layer(x2d, mods4, layer, cond_of_block, seg_len, g_row, win3, bin3, wdw_s, bdw_s, lng, lnb, wout):
    n_tok = x2d.shape[0]
    tm = 512
    cw = 256
    mb = 256
    geo = _conv_geometry(seg_len, tm)
    nch = CONV_E // cw
    const = dict(pipeline_mode=pl.Buffered(1))
    kern = functools.partial(_conformer_kernel, tm=tm, geo=geo, cw=cw, mb=mb)
    return pl.pallas_call(
        kern,
        grid=(n_tok // tm,),
        in_specs=[
            pl.BlockSpec((tm, D_MODEL), lambda i: (i, 0)),
            pl.BlockSpec((1, 1, 3, D_MODEL), lambda i: (layer, cond_of_block(i), 0, 0)),
            pl.BlockSpec((1, D_MODEL), lambda i: (0, 0)),
            pl.BlockSpec((3, nch, D_MODEL, cw), lambda i: (0, 0, 0, 0), **const),
            pl.BlockSpec((3, nch, 1, cw), lambda i: (0, 0, 0, 0)),
            pl.BlockSpec((N_SLABS, 32, LANES), lambda i: (0, 0, 0)),
            pl.BlockSpec((N_SLABS, 1, LANES), lambda i: (0, 0, 0)),
            pl.BlockSpec((1, CONV_E), lambda i: (0, 0)),
            pl.BlockSpec((1, CONV_E), lambda i: (0, 0)),
            pl.BlockSpec((CONV_E, D_MODEL), lambda i: (0, 0), **const),
        ],
        out_specs=pl.BlockSpec((tm, D_MODEL), lambda i: (i, 0)),
        out_shape=jax.ShapeDtypeStruct((n_tok, D_MODEL), F32),
        scratch_shapes=[
            pltpu.VMEM((tm, D_MODEL), BF16),
            pltpu.VMEM((N_SLABS, geo["rows"], LANES), F32),
            pltpu.VMEM((N_SLABS, geo["rows"], LANES), F32),
            pltpu.VMEM((nch, tm, cw), F32),
            pltpu.VMEM((tm, CONV_E), BF16),
        ],
        compiler_params=pltpu.CompilerParams(dimension_semantics=("arbitrary",), vmem_limit_bytes=VMEM_LIMIT),
        name="conformer_seg%d" % seg_len,
    )(x2d, mods4, g_row, win3, bin3, wdw_s, bdw_s, lng, lnb, wout)


def _split3(x):
    hi = x.astype(BF16)
    r1 = x - hi.astype(F32)
    mid = r1.astype(BF16)
    lo = (r1 - mid.astype(F32)).astype(BF16)
    return hi, mid, lo


def _ssd_proj_kernel(x_ref, xp_ref, xn_ref, mod_ref, g_ref, wz_ref, wx_ref, wd_ref, wconv_ref, bconv_ref,
                     dtb_ref, arow_ref, z_ref, act_ref, dtc_ref, h_ref, ext_ref, aslab_ref,
                     *, tm, blocks_per_seq, nw):
    i = pl.program_id(0)
    stride = (tm + 2 * HALO) // SUBLANES
    row0 = SUBLANES + HALO

    @pl.when(i == 0)
    def _():
        ext_ref[...] = jnp.zeros(ext_ref.shape, F32)

    _modulated_norm_to(h_ref, x_ref, g_ref, mod_ref, tm)
    g = g_ref[...]
    sh = mod_ref[0, 0, 0:1, :]
    sc = mod_ref[0, 0, 1:2, :]
    pos = i % blocks_per_seq
    hp = jnp.where(pos > 0, _modulated_norm(xp_ref[...], g, sh, sc), 0.0)
    hn = jnp.where(pos < blocks_per_seq - 1, _modulated_norm(xn_ref[...], g, sh, sc), 0.0)
    h_ref[tm:tm + HALO, :] = hp.astype(BF16)
    h_ref[tm + HALO:tm + 2 * HALO, :] = hn.astype(BF16)

    for m in range(tm // 256):
        rows = slice(m * 256, (m + 1) * 256)
        hm = h_ref[rows, :]
        for j in range(SSD_INNER // nw):
            cs = slice(j * nw, (j + 1) * nw)
            z_ref[rows, cs] = _dot(hm, wz_ref[:, cs])

    t = CHUNK
    row = lax.broadcasted_iota(jnp.int32, (t, t), 0)
    col = lax.broadcasted_iota(jnp.int32, (t, t), 1)
    lmat = jnp.where(col <= row, 1.0, 0.0).astype(BF16)
    umat = jnp.where(col >= row, 1.0, 0.0).astype(BF16)
    for j in range(tm // t):
        rows = slice(j * t, (j + 1) * t)
        dt_all = _softplus(_dot(h_ref[rows, :], wd_ref[...]) + dtb_ref[...])
        parts = _split3(dt_all * arow_ref[...])
        cum_f = _dot(lmat, parts[0]) + _dot(lmat, parts[1]) + _dot(lmat, parts[2])
        cum_b = _dot(umat, parts[0]) + _dot(umat, parts[1]) + _dot(umat, parts[2])
        cum = jnp.where(col < SSD_HEADS, cum_f, cum_b)
        cum_t = cum.T
        dt_t = dt_all.T
        edge = jnp.where(row < SSD_HEADS, cum_t[:, t - 1:t], cum_t[:, 0:1])
        dtc_ref[0, rows, :] = cum
        dtc_ref[1, rows, :] = cum_t
        dtc_ref[2, rows, :] = dt_t
        dtc_ref[3, rows, :] = dt_t * jnp.exp(edge - cum_t)

    m_chunks = [(m0, min(m0 + 256, tm)) for m0 in range(0, tm, 256)]
    m_chunks[-1] = (m_chunks[-1][0], tm + 2 * HALO)
    for (m0, m1) in m_chunks:
        hm = h_ref[m0:m1, :]
        nb = min(m1, tm) - m0
        for j in range(XBC // nw):
            r = _dot(hm, wx_ref[:, j * nw:(j + 1) * nw])
            for q in range(nw // LANES):
                sl = j * (nw // LANES) + q
                ls = slice(q * LANES, (q + 1) * LANES)
                ext_ref[sl, row0 + m0:row0 + m0 + nb, :] = r[0:nb, ls]
                if m1 > tm:
                    ext_ref[sl, SUBLANES:SUBLANES + HALO, :] = r[nb:nb + HALO, ls]
                    ext_ref[sl, row0 + tm:row0 + tm + HALO, :] = r[nb + HALO:nb + 2 * HALO, ls]

    n_trips = 4
    tb = stride // n_trips

    def slab_body(sl, carry):
        wk = [jnp.broadcast_to(wconv_ref[sl, k:k + 1, :], (SUBLANES, LANES)) for k in range(SSD_CONV_K)]
        bias = jnp.broadcast_to(bconv_ref[sl], (SUBLANES, LANES))

        def t_body(ti, c2):
            base = SUBLANES + ti * tb
            accs = [bias] * tb
            for j in range(tb + SSD_CONV_K - 1):
                xrow = ext_ref[sl, pl.ds(base + (j - SSD_CONV_HALF), SUBLANES, stride=stride), :]
                for u in range(max(0, j - SSD_CONV_K + 1), min(tb, j + 1)):
                    accs[u] = accs[u] + wk[j - u] * xrow
            for u in range(tb):
                aslab_ref[sl, pl.ds(base + u, SUBLANES, stride=stride), :] = _silu(accs[u])
            return c2

        lax.fori_loop(0, n_trips, t_body, 0)
        act_ref[sl] = aslab_ref[sl, row0:row0 + tm, :]
        return carry

    lax.fori_loop(0, XBC_SLABS, slab_body, 0)


def _ssd_proj(x2d, mods4, layer, cond_of_block, seq_len, g_row, wz, wx, wd, wconv_s, bconv_s, dtb_row, a_row):
    n_tok = x2d.shape[0]
    tm = min(512, seq_len)
    bps = seq_len // tm
    hb = tm // HALO
    last_hb = n_tok // HALO - 1
    rows = tm + 2 * HALO + 2 * SUBLANES
    const = dict(pipeline_mode=pl.Buffered(1))
    kern = functools.partial(_ssd_proj_kernel, tm=tm, blocks_per_seq=bps, nw=512)
    return pl.pallas_call(
        kern,
        grid=(n_tok // tm,),
        in_specs=[
            pl.BlockSpec((tm, D_MODEL), lambda i: (i, 0)),
            pl.BlockSpec((HALO, D_MODEL), lambda i: (jnp.maximum(i * hb - 1, 0), 0)),
            pl.BlockSpec((HALO, D_MODEL), lambda i: (jnp.minimum((i + 1) * hb, last_hb), 0)),
            pl.BlockSpec((1, 1, 3, D_MODEL), lambda i: (layer, cond_of_block(i), 0, 0)),
            pl.BlockSpec((1, D_MODEL), lambda i: (0, 0)),
            pl.BlockSpec((D_MODEL, SSD_INNER), lambda i: (0, 0), **const),
            pl.BlockSpec((D_MODEL, XBC), lambda i: (0, 0), **const),
            pl.BlockSpec((D_MODEL, LANES), lambda i: (0, 0), **const),
            pl.BlockSpec((XBC_SLABS, SUBLANES, LANES), lambda i: (0, 0, 0)),
            pl.BlockSpec((XBC_SLABS, 1, LANES), lambda i: (0, 0, 0)),
            pl.BlockSpec((1, LANES), lambda i: (0, 0)),
            pl.BlockSpec((1, LANES), lambda i: (0, 0)),
        ],
        out_specs=[
            pl.BlockSpec((tm, SSD_INNER), lambda i: (i, 0)),
            pl.BlockSpec((XBC_SLABS, tm, LANES), lambda i: (0, i, 0)),
            pl.BlockSpec((4, tm, LANES), lambda i: (0, i, 0)),
        ],
        out_shape=[
            jax.ShapeDtypeStruct((n_tok, SSD_INNER), F32),
            jax.ShapeDtypeStruct((XBC_SLABS, n_tok, LANES), F32),
            jax.ShapeDtypeStruct((4, n_tok, LANES), F32),
        ],
        scratch_shapes=[
            pltpu.VMEM((tm + 2 * HALO, D_MODEL), BF16),
            pltpu.VMEM((XBC_SLABS, rows, LANES), F32),
            pltpu.VMEM((XBC_SLABS, rows, LANES), F32),
        ],
        compiler_params=pltpu.CompilerParams(dimension_semantics=("arbitrary",), vmem_limit_bytes=VMEM_LIMIT),
        name="ssd_proj_L%d" % seq_len,
    )(x2d, x2d, x2d, mods4, g_row, wz, wx, wd, wconv_s, bconv_s, dtb_row, a_row)


def _ssd_scan_kernel(*refs, n_chunks, has_h0, emit_state):
    it = iter(refs)
    act_ref, dtc_ref, z_ref, x_ref, mod_ref, fing_ref, dexp_ref, ng_ref, wout_ref = (next(it) for _ in range(9))
    if has_h0:
        h0f_ref, h0b_ref = next(it), next(it)
    o_ref = next(it)
    if emit_state:
        sf_ref, sb_ref = next(it), next(it)
    yb_ref, ht_ref, gt_ref, y_ref, m_ref = (next(it) for _ in range(5))

    t = CHUNK
    s = pl.program_id(1)
    c = pl.program_id(2)
    ci = s * c + (1 - s) * (n_chunks - 1 - c)

    row = lax.broadcasted_iota(jnp.int32, (t, t), 0)
    col = lax.broadcasted_iota(jnp.int32, (t, t), 1)
    lower = col <= row
    upper = col >= row
    lane_lo = col < SSD_HEAD_DIM
    cum_k, cumt_k, dtt_k, w1t_k = 0, 1, 2, 3

    def colb(j):
        return jnp.broadcast_to(dtc_ref[cum_k, :, j:j + 1], (t, LANES))

    def rowb(k, j):
        return jnp.broadcast_to(dtc_ref[k, j:j + 1, :], (t, t))

    def pair_rhs(xs_pair):
        return jnp.concatenate([jnp.where(lane_lo, xs_pair, 0.0).astype(BF16),
                                jnp.where(lane_lo, 0.0, xs_pair).astype(BF16)], axis=0)

    def state_lhs(b_t, j0):
        return jnp.concatenate([(b_t * rowb(w1t_k, j0)).astype(BF16),
                                (b_t * rowb(w1t_k, j0 + 1)).astype(BF16)], axis=1)

    hp_per_g = SSD_HEADS // SSD_GROUPS // 2

    @pl.when(s == 0)
    def _():
        @pl.when(c == 0)
        def _():
            if has_h0:
                gt_ref[...] = h0b_ref[0].T
            else:
                gt_ref[...] = jnp.zeros(gt_ref.shape, F32)

        for g in range(SSD_GROUPS):
            gs = slice(g * 512, (g + 1) * 512)
            c_g = act_ref[C_SLAB0 + g].astype(BF16)
            b_t = act_ref[B_SLAB0 + g].T
            yoff = _dot(c_g, gt_ref[:, gs].astype(BF16))
            for q in range(hp_per_g):
                hp = g * hp_per_g + q
                ps = slice(hp * LANES, (hp + 1) * LANES)
                j0 = SSD_HEADS + 2 * hp
                r_pair = jnp.where(lane_lo, colb(j0), colb(j0 + 1))
                yb_ref[ci, :, ps] = yoff[:, q * LANES:(q + 1) * LANES] * jnp.exp(r_pair)
                upd = _dot(state_lhs(b_t, j0), pair_rhs(act_ref[hp]))
                gt_ref[:, ps] = gt_ref[:, ps] * jnp.exp(r_pair[0:1, :]) + upd

        if emit_state:
            @pl.when(c == n_chunks - 1)
            def _():
                sb_ref[0] = gt_ref[...].T

    @pl.when(s == 1)
    def _():
        @pl.when(c == 0)
        def _():
            if has_h0:
                ht_ref[...] = h0f_ref[0].T
            else:
                ht_ref[...] = jnp.zeros(ht_ref.shape, F32)

        for g in range(SSD_GROUPS):
            gs = slice(g * 512, (g + 1) * 512)
            b_f32 = act_ref[B_SLAB0 + g]
            b_t = b_f32.T
            c_g = act_ref[C_SLAB0 + g].astype(BF16)
            cb = lax.dot_general(c_g, b_f32.astype(BF16), (((1,), (1,)), ((), ())), preferred_element_type=F32)
            y_ref[:, gs] = _dot(c_g, ht_ref[:, gs].astype(BF16))
            for q in range(hp_per_g):
                hp = g * hp_per_g + q
                ps = slice(hp * LANES, (hp + 1) * LANES)
                lhs = []
                col_f = []
                for e in range(2):
                    h = 2 * hp + e
                    hb = SSD_HEADS + h
                    cf = colb(h)
                    col_f.append(cf)
                    dec_f = jnp.where(lower, jnp.exp(cf - rowb(cumt_k, h)), 0.0) * rowb(dtt_k, h)
                    dec_b = jnp.where(upper, jnp.exp(colb(hb) - rowb(cumt_k, hb)), 0.0) * rowb(dtt_k, hb)
                    lhs.append((cb * (dec_f + dec_b)).astype(BF16))
                xs_pair = act_ref[hp]
                rhs = pair_rhs(xs_pair)
                y_diag = _dot(jnp.concatenate(lhs, axis=1), rhs)
                cf_pair = jnp.where(lane_lo, col_f[0], col_f[1])
                y_ref[:, ps] = (y_diag + y_ref[:, ps] * jnp.exp(cf_pair)
                                + yb_ref[ci, :, ps] + dexp_ref[:, ps] * xs_pair)
                upd = _dot(state_lhs(b_t, 2 * hp), rhs)
                ht_ref[:, ps] = ht_ref[:, ps] * jnp.exp(cf_pair[t - 1:t, :]) + upd

        if emit_state:
            @pl.when(c == n_chunks - 1)
            def _():
                sf_ref[0] = ht_ref[...].T

        gw = SSD_INNER // SSD_GROUPS
        for k in range(SSD_GROUPS):
            ks = slice(k * gw, (k + 1) * gw)
            yg = y_ref[:, ks] * _silu(z_ref[:, ks])
            ms = jnp.mean(yg * yg, axis=-1, keepdims=True)
            m_ref[:, ks] = (yg * lax.rsqrt(ms + EPS) * ng_ref[:, ks]).astype(BF16)
        out = _dot(m_ref[...], wout_ref[...])
        ynew = x_ref[...] + mod_ref[0, 0, 2:3, :] * out
        ms = jnp.mean(ynew * ynew, axis=-1, keepdims=True)
        o_ref[...] = ynew * lax.rsqrt(ms + EPS) * fing_ref[...]


def _ssd_scan(x2d, z, act, dtc, mods4, layer, cond_of_batch, seq_len, fin_g, d_exp, ng_row, wout, h0f, h0b,
              emit_state):
    n_tok = x2d.shape[0]
    n_batch = n_tok // seq_len
    nc = seq_len // CHUNK
    has_h0 = h0f is not None

    def blk(b, s, c):
        return b * nc + s * c + (1 - s) * (nc - 1 - c)

    const = dict(pipeline_mode=pl.Buffered(1))
    hp = SSD_HEADS * SSD_HEAD_DIM
    in_specs = [
        pl.BlockSpec((XBC_SLABS, CHUNK, LANES), lambda b, s, c: (0, blk(b, s, c), 0)),
        pl.BlockSpec((4, CHUNK, LANES), lambda b, s, c: (0, blk(b, s, c), 0)),
        pl.BlockSpec((CHUNK, SSD_INNER), lambda b, s, c: (b * nc + s * c, 0)),
        pl.BlockSpec((CHUNK, D_MODEL), lambda b, s, c: (b * nc + s * c, 0)),
        pl.BlockSpec((1, 1, 3, D_MODEL), lambda b, s, c: (layer, cond_of_batch(b), 0, 0)),
        pl.BlockSpec((1, D_MODEL), lambda b, s, c: (0, 0)),
        pl.BlockSpec((1, SSD_INNER), lambda b, s, c: (0, 0)),
        pl.BlockSpec((1, SSD_INNER), lambda b, s, c: (0, 0)),
        pl.BlockSpec((SSD_INNER, D_MODEL), lambda b, s, c: (0, 0), **const),
    ]
    args = [act, dtc, z, x2d, mods4, fin_g, d_exp, ng_row, wout]
    if has_h0:
        in_specs += [pl.BlockSpec((1, hp, D_STATE), lambda b, s, c: (b, 0, 0))] * 2
        args += [h0f, h0b]
    out_specs = [pl.BlockSpec((CHUNK, D_MODEL), lambda b, s, c: (b * nc + s * c, 0))]
    out_shape = [jax.ShapeDtypeStruct((n_tok, D_MODEL), F32)]
    if emit_state:
        out_specs += [pl.BlockSpec((1, hp, D_STATE), lambda b, s, c: (b, 0, 0))] * 2
        out_shape += [jax.ShapeDtypeStruct((n_batch, hp, D_STATE), F32)] * 2
    kern = functools.partial(_ssd_scan_kernel, n_chunks=nc, has_h0=has_h0, emit_state=emit_state)
    return pl.pallas_call(
        kern,
        grid=(n_batch, 2, nc),
        in_specs=in_specs,
        out_specs=out_specs,
        out_shape=out_shape,
        scratch_shapes=[
            pltpu.VMEM((nc, CHUNK, SSD_INNER), F32),
            pltpu.VMEM((D_STATE, hp), F32),
            pltpu.VMEM((D_STATE, hp), F32),
            pltpu.VMEM((CHUNK, SSD_INNER), F32),
            pltpu.VMEM((CHUNK, SSD_INNER), BF16),
        ],
        compiler_params=pltpu.CompilerParams(dimension_semantics=("arbitrary", "arbitrary", "arbitrary"),
                                             vmem_limit_bytes=VMEM_LIMIT),
        name="ssd_scan_L%d" % seq_len,
    )(*args)


def kernel(x_prompt, x_sample, state_fwd, state_bwd, c, c_ctx, ada_w, ada_b, norm_g, conv_w_in, conv_b_in, conv_w_dw, conv_b_dw, conv_ln_g, conv_ln_b, conv_w_out, ssd_w_in, ssd_w_conv, ssd_b_conv, ssd_dt_bias_f, ssd_dt_bias_b, ssd_a_log_f, ssd_a_log_b, ssd_d, ssd_norm_g, ssd_w_out, final_norm_g):
    nb_p, seq_p, _ = x_prompt.shape
    nb_s, seq_s, _ = x_sample.shape
    grid_w = 64

    cond = jnp.concatenate([c_ctx[None, :], c, jnp.zeros((16 - 1 - nb_s, D_MODEL), F32)], axis=0)
    mods4 = _adaln(cond, ada_w, ada_b).reshape(ada_w.shape[0], 16, 3, D_MODEL)

    xp = x_prompt.reshape(nb_p * seq_p, D_MODEL)
    xs = x_sample.reshape(nb_s * seq_s, D_MODEL)

    cw = 256
    nch = CONV_E // cw
    win3 = conv_w_in[0].reshape(D_MODEL, 3, nch, cw).transpose(1, 2, 0, 3).astype(BF16)
    bin3 = conv_b_in[0].reshape(3, nch, 1, cw)
    wdw_s = jnp.pad(conv_w_dw[0], ((0, 32 - CONV_K), (0, 0))).reshape(32, N_SLABS, LANES).transpose(1, 0, 2)
    bdw_s = conv_b_dw[0].reshape(N_SLABS, 1, LANES)
    lng = conv_ln_g[0].reshape(1, CONV_E)
    lnb = conv_ln_b[0].reshape(1, CONV_E)
    wout0 = conv_w_out[0].astype(BF16)
    g0 = norm_g[0].reshape(1, D_MODEL)
    tm = 512
    xp = _conformer_layer(xp, mods4, 0, lambda i: 0, seq_p, g0, win3, bin3, wdw_s, bdw_s, lng, lnb, wout0)
    xs = _conformer_layer(xs, mods4, 0, lambda i: 1 + i // (seq_s // tm), grid_w, g0, win3, bin3, wdw_s, bdw_s,
                          lng, lnb, wout0)

    w_in = ssd_w_in[0]
    wz = w_in[:, :SSD_INNER].astype(BF16)
    wx = w_in[:, SSD_INNER:SSD_INNER + XBC].astype(BF16)
    wd = jnp.pad(w_in[:, SSD_INNER + XBC:], ((0, 0), (0, LANES - 2 * SSD_HEADS))).astype(BF16)
    g1 = norm_g[1].reshape(1, D_MODEL)
    wconv_s = jnp.pad(ssd_w_conv[0], ((0, SUBLANES - SSD_CONV_K), (0, 0))).reshape(
        SUBLANES, XBC_SLABS, LANES).transpose(1, 0, 2)
    bconv_s = ssd_b_conv[0].reshape(XBC_SLABS, 1, LANES)
    zpad = jnp.zeros((LANES - 2 * SSD_HEADS,), F32)
    dtb_row = jnp.concatenate([ssd_dt_bias_f[0], ssd_dt_bias_b[0], zpad]).reshape(1, LANES)
    a_row = jnp.concatenate([-jnp.exp(ssd_a_log_f[0]), -jnp.exp(ssd_a_log_b[0]), zpad]).reshape(1, LANES)
    d_exp = jnp.repeat(ssd_d[0], SSD_HEAD_DIM).reshape(1, SSD_INNER)
    ng_row = ssd_norm_g[0].reshape(1, SSD_INNER)
    wout1 = ssd_w_out[0].astype(BF16)
    fin_g = final_norm_g.reshape(1, D_MODEL)
    hp = SSD_HEADS * SSD_HEAD_DIM

    zp, actp, dtcp = _ssd_proj(xp, mods4, 1, lambda i: 0, seq_p, g1, wz, wx, wd, wconv_s, bconv_s, dtb_row, a_row)
    zs_, acts, dtcs = _ssd_proj(xs, mods4, 1, lambda i: 1 + i // (seq_s // tm), seq_s, g1, wz, wx, wd, wconv_s,
                                bconv_s, dtb_row, a_row)

    yp, sf, sb = _ssd_scan(xp, zp, actp, dtcp, mods4, 1, lambda b: 0, seq_p, fin_g, d_exp, ng_row, wout1,
                           None, None, True)
    (ys,) = _ssd_scan(xs, zs_, acts, dtcs, mods4, 1, lambda b: 1 + b, seq_s, fin_g, d_exp, ng_row, wout1,
                      state_fwd[:, 0].reshape(nb_s, hp, D_STATE), state_bwd[:, 0].reshape(nb_s, hp, D_STATE), False)

    y_prompt = yp.reshape(nb_p, seq_p, D_MODEL)
    y_sample = ys.reshape(nb_s, seq_s, D_MODEL)
    new_f = sf.reshape(nb_p, 1, SSD_HEADS, SSD_HEAD_DIM, D_STATE)
    new_b = sb.reshape(nb_p, 1, SSD_HEADS, SSD_HEAD_DIM, D_STATE)
    return (y_prompt, y_sample, new_f, new_b)
```

```python
import functools

import jax
import jax.numpy as jnp
from jax import lax
from jax.experimental import pallas as pl
from jax.experimental.pallas import tpu as pltpu

F32 = jnp.float32
BF16 = jnp.bfloat16

D_MODEL = 1024
CONV_E = 2048
CONV_K = 31
CONV_HALF = CONV_K // 2
SSD_INNER = 2048
SSD_HEADS = 32
SSD_HEAD_DIM = 64
SSD_GROUPS = 4
D_STATE = 128
SSD_CONV_K = 5
SSD_CONV_HALF = SSD_CONV_K // 2
XBC = SSD_INNER + 2 * SSD_GROUPS * D_STATE
CHUNK = 128
EPS = 1e-6

LANES = 128
SUBLANES = 8
N_SLABS = CONV_E // LANES
XBC_SLABS = XBC // LANES
X_SLABS = SSD_INNER // LANES
B_SLAB0 = X_SLABS
C_SLAB0 = X_SLABS + SSD_GROUPS
PAD = 16
HALO = 16
ROW_CHUNK = 64
VMEM_LIMIT = 56 * 1024 * 1024


def _sigmoid(x):
    return jax.nn.sigmoid(x)


def _silu(x):
    return x * _sigmoid(x)


def _softplus(x):
    return jnp.maximum(x, 0.0) + jnp.log1p(jnp.exp(-jnp.abs(x)))


def _dot(a, b):
    return jnp.dot(a, b, preferred_element_type=F32)


def _adaln_kernel(cond_ref, w_ref, b_ref, o_ref):
    s = _silu(cond_ref[...])
    o_ref[0] = _dot(s.astype(BF16), w_ref[0].astype(BF16)) + b_ref[0]


def _adaln(cond16, ada_w, ada_b):
    depth = ada_w.shape[0]
    nt = 4
    tn = 3 * D_MODEL // nt
    return pl.pallas_call(
        _adaln_kernel,
        grid=(depth, nt),
        in_specs=[
            pl.BlockSpec((16, D_MODEL), lambda i, j: (0, 0)),
            pl.BlockSpec((1, D_MODEL, tn), lambda i, j: (i, 0, j)),
            pl.BlockSpec((1, 1, tn), lambda i, j: (i, 0, j)),
        ],
        out_specs=pl.BlockSpec((1, 16, tn), lambda i, j: (i, 0, j)),
        out_shape=jax.ShapeDtypeStruct((depth, 16, 3 * D_MODEL), F32),
        name="adaln",
    )(cond16, ada_w, ada_b.reshape(depth, 1, 3 * D_MODEL))


def _modulated_norm(x, g, sh, sc):
    ms = jnp.mean(x * x, axis=-1, keepdims=True)
    return (x * lax.rsqrt(ms + EPS) * g) * (1.0 + sc) + sh


def _modulated_norm_to(h_ref, x_ref, g_ref, mod_ref, tm):
    g = g_ref[...]
    sh = mod_ref[0, 0, 0:1, :]
    sc = mod_ref[0, 0, 1:2, :]

    def body(r, carry):
        rows = pl.ds(pl.multiple_of(r * 32, 32), 32)
        h_ref[rows, :] = _modulated_norm(x_ref[rows, :], g, sh, sc).astype(BF16)
        return carry

    lax.fori_loop(0, tm // 32, body, 0, unroll=4)


def _conv_geometry(seg_len, tm):
    n_seg = tm // seg_len
    if seg_len == 64:
        assert n_seg == SUBLANES
        pitch, stride, n_grp, grp_pitch, n_t0, t_unroll = 84, 84, 1, 0, seg_len, 16
    else:
        assert seg_len == 256
        pitch, stride, n_grp, grp_pitch, n_t0, t_unroll = 304, 36, n_seg, 304, 36, 12
    rows = PAD + n_seg * pitch
    rows = -(-(rows + PAD) // SUBLANES) * SUBLANES if seg_len == 64 else rows
    pieces = []
    for p in range(tm // ROW_CHUNK):
        t = p * ROW_CHUNK
        pieces.append((t, PAD + pitch * (t // seg_len) + (t % seg_len)))
    assert n_t0 % t_unroll == 0
    return dict(pitch=pitch, stride=stride, n_grp=n_grp, grp_pitch=grp_pitch, n_t0=n_t0, rows=rows,
                t_unroll=t_unroll, pieces=tuple(pieces))


def _conformer_kernel(x_ref, mod_ref, g_ref, win_ref, bin_ref, wdw_ref, bdw_ref, lng_ref, lnb_ref, wout_ref,
                      o_ref, h_ref, cina_ref, cinb_ref, cout_ref, zs_ref, m_ref, *, tm, geo, cw, mb):
    nch = CONV_E // cw
    spc = cw // LANES
    stride = geo["stride"]
    tb = geo["t_unroll"]
    assert tm // mb == spc and nch % 2 == 0

    @pl.when(pl.program_id(0) == 0)
    def _():
        cina_ref[...] = jnp.zeros(cina_ref.shape, F32)
        cinb_ref[...] = jnp.zeros(cinb_ref.shape, F32)

    _modulated_norm_to(h_ref, x_ref, g_ref, mod_ref, tm)

    def proj_part(n, m, cin_ref):
        hm = h_ref[m * mb:(m + 1) * mb, :]
        a = _dot(hm, win_ref[0, n]) + bin_ref[0, n]
        gt = _dot(hm, win_ref[1, n]) + bin_ref[1, n]
        v = a * _sigmoid(gt)
        for q in range(spc):
            for (t, crow) in geo["pieces"]:
                if m * mb <= t < (m + 1) * mb:
                    cin_ref[q, pl.ds(crow, ROW_CHUNK), :] = (
                        v[t - m * mb:t - m * mb + ROW_CHUNK, q * LANES:(q + 1) * LANES])
        z = _dot(hm, win_ref[2, n]) + bin_ref[2, n]
        zs_ref[n, m * mb:(m + 1) * mb, :] = _silu(z)

    def conv_part(n, q, cin_ref):
        sl = n * spc + q
        wk = [jnp.broadcast_to(wdw_ref[sl, k:k + 1, :], (SUBLANES, LANES)) for k in range(CONV_K)]
        bias = jnp.broadcast_to(bdw_ref[sl], (SUBLANES, LANES))
        for grp in range(geo["n_grp"]):
            for ti in range(geo["n_t0"] // tb):
                base = PAD + grp * geo["grp_pitch"] + ti * tb
                accs = [bias] * tb
                for j in range(tb + CONV_K - 1):
                    xrow = cin_ref[q, pl.ds(base + (j - CONV_HALF), SUBLANES, stride=stride), :]
                    for u in range(max(0, j - CONV_K + 1), min(tb, j + 1)):
                        accs[u] = accs[u] + wk[j - u] * xrow
                for u in range(tb):
                    cout_ref[sl, pl.ds(base + u, SUBLANES, stride=stride), :] = accs[u]

    def proj_and_conv(n_proj, cin_proj, n_conv, cin_conv):
        for part in range(spc):
            proj_part(n_proj, part, cin_proj)
            conv_part(n_conv, part, cin_conv)

    for m in range(spc):
        proj_part(0, m, cina_ref)

    def pair_body(i, carry):
        n = 2 * i + 1
        proj_and_conv(n, cinb_ref, n - 1, cina_ref)
        proj_and_conv(n + 1, cina_ref, n, cinb_ref)
        return carry

    lax.fori_loop(0, nch // 2 - 1, pair_body, 0)
    proj_and_conv(nch - 1, cinb_ref, nch - 2, cina_ref)
    for q in range(spc):
        conv_part(nch - 1, q, cinb_ref)

    for (t, crow) in geo["pieces"]:
        rows = pl.ds(crow, ROW_CHUNK)
        s1 = cout_ref[0, rows, :]
        for sl in range(1, N_SLABS):
            s1 = s1 + cout_ref[sl, rows, :]
        mu = jnp.sum(s1, axis=-1, keepdims=True) * (1.0 / CONV_E)
        s2 = jnp.zeros((ROW_CHUNK, LANES), F32)
        for sl in range(N_SLABS):
            d = cout_ref[sl, rows, :] - mu
            s2 = s2 + d * d
        var = jnp.sum(s2, axis=-1, keepdims=True) * (1.0 / CONV_E)
        rstd = lax.rsqrt(var + EPS)
        for sl in range(N_SLABS):
            cs = slice(sl * LANES, (sl + 1) * LANES)
            yn = (cout_ref[sl, rows, :] - mu) * rstd * lng_ref[:, cs] + lnb_ref[:, cs]
            zq = zs_ref[sl * LANES // cw, t:t + ROW_CHUNK, (sl * LANES) % cw:(sl * LANES) % cw + LANES]
            m_ref[t:t + ROW_CHUNK, cs] = (_silu(yn) * zq).astype(BF16)

    gate = mod_ref[0, 0, 2:3, :]
    for m in range(tm // mb):
        rows = slice(m * mb, (m + 1) * mb)
        out = _dot(m_ref[rows, :], wout_ref[...])
        o_ref[rows, :] = x_ref[rows, :] + gate * out


def _conformer_layer(x2d, mods4, layer, cond_of_block, seg_len, g_row, win3, bin3, wdw_s, bdw_s, lng, lnb, wout):
    n_tok = x2d.shape[0]
    tm = 512
    cw = 256
    mb = 256
    geo = _conv_geometry(seg_len, tm)
    nch = CONV_E // cw
    const = dict(pipeline_mode=pl.Buffered(1))
    kern = functools.partial(_conformer_kernel, tm=tm, geo=geo, cw=cw, mb=mb)
    return pl.pallas_call(
        kern,
        grid=(n_tok // tm,),
        in_specs=[
            pl.BlockSpec((tm, D_MODEL), lambda i: (i, 0)),
            pl.BlockSpec((1, 1, 3, D_MODEL), lambda i: (layer, cond_of_block(i), 0, 0)),
            pl.BlockSpec((1, D_MODEL), lambda i: (0, 0)),
            pl.BlockSpec((3, nch, D_MODEL, cw), lambda i: (0, 0, 0, 0), **const),
            pl.BlockSpec((3, nch, 1, cw), lambda i: (0, 0, 0, 0)),
            pl.BlockSpec((N_SLABS, 32, LANES), lambda i: (0, 0, 0)),
            pl.BlockSpec((N_SLABS, 1, LANES), lambda i: (0, 0, 0)),
            pl.BlockSpec((1, CONV_E), lambda i: (0, 0)),
            pl.BlockSpec((1, CONV_E), lambda i: (0, 0)),
            pl.BlockSpec((CONV_E, D_MODEL), lambda i: (0, 0), **const),
        ],
        out_specs=pl.BlockSpec((tm, D_MODEL), lambda i: (i, 0)),
        out_shape=jax.ShapeDtypeStruct((n_tok, D_MODEL), F32),
        scratch_shapes=[
            pltpu.VMEM((tm, D_MODEL), BF16),
            pltpu.VMEM((cw // LANES, geo["rows"], LANES), F32),
            pltpu.VMEM((cw // LANES, geo["rows"], LANES), F32),
            pltpu.VMEM((N_SLABS, geo["rows"], LANES), F32),
            pltpu.VMEM((nch, tm, cw), F32),
            pltpu.VMEM((tm, CONV_E), BF16),
        ],
        compiler_params=pltpu.CompilerParams(dimension_semantics=("arbitrary",), vmem_limit_bytes=VMEM_LIMIT),
        name="conformer_seg%d" % seg_len,
    )(x2d, mods4, g_row, win3, bin3, wdw_s, bdw_s, lng, lnb, wout)


def _split3(x):
    hi = x.astype(BF16)
    r1 = x - hi.astype(F32)
    mid = r1.astype(BF16)
    lo = (r1 - mid.astype(F32)).astype(BF16)
    return hi, mid, lo


def _ssd_proj_kernel(x_ref, xp_ref, xn_ref, mod_ref, g_ref, wz_ref, wx_ref, wd_ref, wconv_ref, bconv_ref,
                     dtb_ref, arow_ref, z_ref, act_ref, dtc_ref, h_ref, ext_ref, aslab_ref,
                     *, tm, blocks_per_seq, nw):
    i = pl.program_id(0)
    stride = (tm + 2 * HALO) // SUBLANES
    row0 = SUBLANES + HALO

    @pl.when(i == 0)
    def _():
        ext_ref[...] = jnp.zeros(ext_ref.shape, F32)

    _modulated_norm_to(h_ref, x_ref, g_ref, mod_ref, tm)
    g = g_ref[...]
    sh = mod_ref[0, 0, 0:1, :]
    sc = mod_ref[0, 0, 1:2, :]
    pos = i % blocks_per_seq
    hp = jnp.where(pos > 0, _modulated_norm(xp_ref[...], g, sh, sc), 0.0)
    hn = jnp.where(pos < blocks_per_seq - 1, _modulated_norm(xn_ref[...], g, sh, sc), 0.0)
    h_ref[tm:tm + HALO, :] = hp.astype(BF16)
    h_ref[tm + HALO:tm + 2 * HALO, :] = hn.astype(BF16)

    for m in range(tm // 256):
        rows = slice(m * 256, (m + 1) * 256)
        hm = h_ref[rows, :]
        for j in range(SSD_INNER // nw):
            cs = slice(j * nw, (j + 1) * nw)
            z_ref[rows, cs] = _dot(hm, wz_ref[:, cs])

    t = CHUNK
    row = lax.broadcasted_iota(jnp.int32, (t, t), 0)
    col = lax.broadcasted_iota(jnp.int32, (t, t), 1)
    lmat = jnp.where(col <= row, 1.0, 0.0).astype(BF16)
    umat = jnp.where(col >= row, 1.0, 0.0).astype(BF16)
    for j in range(tm // t):
        rows = slice(j * t, (j + 1) * t)
        dt_all = _softplus(_dot(h_ref[rows, :], wd_ref[...]) + dtb_ref[...])
        parts = _split3(dt_all * arow_ref[...])
        cum_f = _dot(lmat, parts[0]) + _dot(lmat, parts[1]) + _dot(lmat, parts[2])
        cum_b = _dot(umat, parts[0]) + _dot(umat, parts[1]) + _dot(umat, parts[2])
        cum = jnp.where(col < SSD_HEADS, cum_f, cum_b)
        cum_t = cum.T
        dt_t = dt_all.T
        edge = jnp.where(row < SSD_HEADS, cum_t[:, t - 1:t], cum_t[:, 0:1])
        dtc_ref[0, rows, :] = cum
        dtc_ref[1, rows, :] = cum_t
        dtc_ref[2, rows, :] = dt_t
        dtc_ref[3, rows, :] = dt_t * jnp.exp(edge - cum_t)

    m_chunks = [(m0, min(m0 + 256, tm)) for m0 in range(0, tm, 256)]
    m_chunks[-1] = (m_chunks[-1][0], tm + 2 * HALO)
    for (m0, m1) in m_chunks:
        hm = h_ref[m0:m1, :]
        nb = min(m1, tm) - m0
        for j in range(XBC // nw):
            r = _dot(hm, wx_ref[:, j * nw:(j + 1) * nw])
            for q in range(nw // LANES):
                sl = j * (nw // LANES) + q
                ls = slice(q * LANES, (q + 1) * LANES)
                ext_ref[sl, row0 + m0:row0 + m0 + nb, :] = r[0:nb, ls]
                if m1 > tm:
                    ext_ref[sl, SUBLANES:SUBLANES + HALO, :] = r[nb:nb + HALO, ls]
                    ext_ref[sl, row0 + tm:row0 + tm + HALO, :] = r[nb + HALO:nb + 2 * HALO, ls]

    n_trips = 4
    tb = stride // n_trips

    def slab_body(sl, carry):
        wk = [jnp.broadcast_to(wconv_ref[sl, k:k + 1, :], (SUBLANES, LANES)) for k in range(SSD_CONV_K)]
        bias = jnp.broadcast_to(bconv_ref[sl], (SUBLANES, LANES))

        def t_body(ti, c2):
            base = SUBLANES + ti * tb
            accs = [bias] * tb
            for j in range(tb + SSD_CONV_K - 1):
                xrow = ext_ref[sl, pl.ds(base + (j - SSD_CONV_HALF), SUBLANES, stride=stride), :]
                for u in range(max(0, j - SSD_CONV_K + 1), min(tb, j + 1)):
                    accs[u] = accs[u] + wk[j - u] * xrow
            for u in range(tb):
                aslab_ref[sl, pl.ds(base + u, SUBLANES, stride=stride), :] = _silu(accs[u])
            return c2

        lax.fori_loop(0, n_trips, t_body, 0)
        act_ref[sl] = aslab_ref[sl, row0:row0 + tm, :]
        return carry

    lax.fori_loop(0, XBC_SLABS, slab_body, 0)


def _ssd_proj(x2d, mods4, layer, cond_of_block, seq_len, g_row, wz, wx, wd, wconv_s, bconv_s, dtb_row, a_row):
    n_tok = x2d.shape[0]
    tm = min(512, seq_len)
    bps = seq_len // tm
    hb = tm // HALO
    last_hb = n_tok // HALO - 1
    rows = tm + 2 * HALO + 2 * SUBLANES
    const = dict(pipeline_mode=pl.Buffered(1))
    kern = functools.partial(_ssd_proj_kernel, tm=tm, blocks_per_seq=bps, nw=512)
    return pl.pallas_call(
        kern,
        grid=(n_tok // tm,),
        in_specs=[
            pl.BlockSpec((tm, D_MODEL), lambda i: (i, 0)),
            pl.BlockSpec((HALO, D_MODEL), lambda i: (jnp.maximum(i * hb - 1, 0), 0)),
            pl.BlockSpec((HALO, D_MODEL), lambda i: (jnp.minimum((i + 1) * hb, last_hb), 0)),
            pl.BlockSpec((1, 1, 3, D_MODEL), lambda i: (layer, cond_of_block(i), 0, 0)),
            pl.BlockSpec((1, D_MODEL), lambda i: (0, 0)),
            pl.BlockSpec((D_MODEL, SSD_INNER), lambda i: (0, 0), **const),
            pl.BlockSpec((D_MODEL, XBC), lambda i: (0, 0), **const),
            pl.BlockSpec((D_MODEL, LANES), lambda i: (0, 0), **const),
            pl.BlockSpec((XBC_SLABS, SUBLANES, LANES), lambda i: (0, 0, 0)),
            pl.BlockSpec((XBC_SLABS, 1, LANES), lambda i: (0, 0, 0)),
            pl.BlockSpec((1, LANES), lambda i: (0, 0)),
            pl.BlockSpec((1, LANES), lambda i: (0, 0)),
        ],
        out_specs=[
            pl.BlockSpec((tm, SSD_INNER), lambda i: (i, 0)),
            pl.BlockSpec((XBC_SLABS, tm, LANES), lambda i: (0, i, 0)),
            pl.BlockSpec((4, tm, LANES), lambda i: (0, i, 0)),
        ],
        out_shape=[
            jax.ShapeDtypeStruct((n_tok, SSD_INNER), F32),
            jax.ShapeDtypeStruct((XBC_SLABS, n_tok, LANES), F32),
            jax.ShapeDtypeStruct((4, n_tok, LANES), F32),
        ],
        scratch_shapes=[
            pltpu.VMEM((tm + 2 * HALO, D_MODEL), BF16),
            pltpu.VMEM((XBC_SLABS, rows, LANES), F32),
            pltpu.VMEM((XBC_SLABS, rows, LANES), F32),
        ],
        compiler_params=pltpu.CompilerParams(dimension_semantics=("arbitrary",), vmem_limit_bytes=VMEM_LIMIT),
        name="ssd_proj_L%d" % seq_len,
    )(x2d, x2d, x2d, mods4, g_row, wz, wx, wd, wconv_s, bconv_s, dtb_row, a_row)


def _ssd_scan_kernel(*refs, n_chunks, has_h0, emit_state):
    it = iter(refs)
    act_ref, dtc_ref, z_ref, x_ref, mod_ref, fing_ref, dexp_ref, ng_ref, wout_ref = (next(it) for _ in range(9))
    if has_h0:
        h0f_ref, h0b_ref = next(it), next(it)
    o_ref = next(it)
    if emit_state:
        sf_ref, sb_ref = next(it), next(it)
    yb_ref, ht_ref, gt_ref, y_ref, m_ref = (next(it) for _ in range(5))

    t = CHUNK
    s = pl.program_id(1)
    c = pl.program_id(2)
    ci = s * c + (1 - s) * (n_chunks - 1 - c)

    row = lax.broadcasted_iota(jnp.int32, (t, t), 0)
    col = lax.broadcasted_iota(jnp.int32, (t, t), 1)
    lower = col <= row
    upper = col >= row
    lane_lo = col < SSD_HEAD_DIM
    cum_k, cumt_k, dtt_k, w1t_k = 0, 1, 2, 3

    def colb(j):
        return jnp.broadcast_to(dtc_ref[cum_k, :, j:j + 1], (t, LANES))

    def rowb(k, j):
        return jnp.broadcast_to(dtc_ref[k, j:j + 1, :], (t, t))

    def pair_rhs(xs_pair):
        return jnp.concatenate([jnp.where(lane_lo, xs_pair, 0.0).astype(BF16),
                                jnp.where(lane_lo, 0.0, xs_pair).astype(BF16)], axis=0)

    def state_lhs(b_t, j0):
        return jnp.concatenate([(b_t * rowb(w1t_k, j0)).astype(BF16),
                                (b_t * rowb(w1t_k, j0 + 1)).astype(BF16)], axis=1)

    hp_per_g = SSD_HEADS // SSD_GROUPS // 2

    @pl.when(s == 0)
    def _():
        @pl.when(c == 0)
        def _():
            if has_h0:
                gt_ref[...] = h0b_ref[0].T
            else:
                gt_ref[...] = jnp.zeros(gt_ref.shape, F32)

        for g in range(SSD_GROUPS):
            gs = slice(g * 512, (g + 1) * 512)
            c_g = act_ref[C_SLAB0 + g].astype(BF16)
            b_t = act_ref[B_SLAB0 + g].T
            yoff = _dot(c_g, gt_ref[:, gs].astype(BF16))
            for q in range(hp_per_g):
                hp = g * hp_per_g + q
                ps = slice(hp * LANES, (hp + 1) * LANES)
                j0 = SSD_HEADS + 2 * hp
                r_pair = jnp.where(lane_lo, colb(j0), colb(j0 + 1))
                yb_ref[ci, :, ps] = yoff[:, q * LANES:(q + 1) * LANES] * jnp.exp(r_pair)
                upd = _dot(state_lhs(b_t, j0), pair_rhs(act_ref[hp]))
                gt_ref[:, ps] = gt_ref[:, ps] * jnp.exp(r_pair[0:1, :]) + upd

        if emit_state:
            @pl.when(c == n_chunks - 1)
            def _():
                sb_ref[0] = gt_ref[...].T

    @pl.when(s == 1)
    def _():
        @pl.when(c == 0)
        def _():
            if has_h0:
                ht_ref[...] = h0f_ref[0].T
            else:
                ht_ref[...] = jnp.zeros(ht_ref.shape, F32)

        for g in range(SSD_GROUPS):
            gs = slice(g * 512, (g + 1) * 512)
            b_f32 = act_ref[B_SLAB0 + g]
            b_t = b_f32.T
            c_g = act_ref[C_SLAB0 + g].astype(BF16)
            cb = lax.dot_general(c_g, b_f32.astype(BF16), (((1,), (1,)), ((), ())), preferred_element_type=F32)
            y_ref[:, gs] = _dot(c_g, ht_ref[:, gs].astype(BF16))
            for q in range(hp_per_g):
                hp = g * hp_per_g + q
                ps = slice(hp * LANES, (hp + 1) * LANES)
                lhs = []
                col_f = []
                for e in range(2):
                    h = 2 * hp + e
                    hb = SSD_HEADS + h
                    cf = colb(h)
                    col_f.append(cf)
                    dec_f = jnp.where(lower, jnp.exp(cf - rowb(cumt_k, h)), 0.0) * rowb(dtt_k, h)
                    dec_b = jnp.where(upper, jnp.exp(colb(hb) - rowb(cumt_k, hb)), 0.0) * rowb(dtt_k, hb)
                    lhs.append((cb * (dec_f + dec_b)).astype(BF16))
                xs_pair = act_ref[hp]
                rhs = pair_rhs(xs_pair)
                y_diag = _dot(jnp.concatenate(lhs, axis=1), rhs)
                cf_pair = jnp.where(lane_lo, col_f[0], col_f[1])
                y_ref[:, ps] = (y_diag + y_ref[:, ps] * jnp.exp(cf_pair)
                                + yb_ref[ci, :, ps] + dexp_ref[:, ps] * xs_pair)
                upd = _dot(state_lhs(b_t, 2 * hp), rhs)
                ht_ref[:, ps] = ht_ref[:, ps] * jnp.exp(cf_pair[t - 1:t, :]) + upd

        if emit_state:
            @pl.when(c == n_chunks - 1)
            def _():
                sf_ref[0] = ht_ref[...].T

        gw = SSD_INNER // SSD_GROUPS
        for k in range(SSD_GROUPS):
            ks = slice(k * gw, (k + 1) * gw)
            yg = y_ref[:, ks] * _silu(z_ref[:, ks])
            ms = jnp.mean(yg * yg, axis=-1, keepdims=True)
            m_ref[:, ks] = (yg * lax.rsqrt(ms + EPS) * ng_ref[:, ks]).astype(BF16)
        out = _dot(m_ref[...], wout_ref[...])
        ynew = x_ref[...] + mod_ref[0, 0, 2:3, :] * out
        ms = jnp.mean(ynew * ynew, axis=-1, keepdims=True)
        o_ref[...] = ynew * lax.rsqrt(ms + EPS) * fing_ref[...]


def _ssd_scan(x2d, z, act, dtc, mods4, layer, cond_of_batch, seq_len, fin_g, d_exp, ng_row, wout, h0f, h0b,
              emit_state):
    n_tok = x2d.shape[0]
    n_batch = n_tok // seq_len
    nc = seq_len // CHUNK
    has_h0 = h0f is not None

    def blk(b, s, c):
        return b * nc + s * c + (1 - s) * (nc - 1 - c)

    const = dict(pipeline_mode=pl.Buffered(1))
    hp = SSD_HEADS * SSD_HEAD_DIM
    in_specs = [
        pl.BlockSpec((XBC_SLABS, CHUNK, LANES), lambda b, s, c: (0, blk(b, s, c), 0)),
        pl.BlockSpec((4, CHUNK, LANES), lambda b, s, c: (0, blk(b, s, c), 0)),
        pl.BlockSpec((CHUNK, SSD_INNER), lambda b, s, c: (b * nc + s * c, 0)),
        pl.BlockSpec((CHUNK, D_MODEL), lambda b, s, c: (b * nc + s * c, 0)),
        pl.BlockSpec((1, 1, 3, D_MODEL), lambda b, s, c: (layer, cond_of_batch(b), 0, 0)),
        pl.BlockSpec((1, D_MODEL), lambda b, s, c: (0, 0)),
        pl.BlockSpec((1, SSD_INNER), lambda b, s, c: (0, 0)),
        pl.BlockSpec((1, SSD_INNER), lambda b, s, c: (0, 0)),
        pl.BlockSpec((SSD_INNER, D_MODEL), lambda b, s, c: (0, 0), **const),
    ]
    args = [act, dtc, z, x2d, mods4, fin_g, d_exp, ng_row, wout]
    if has_h0:
        in_specs += [pl.BlockSpec((1, hp, D_STATE), lambda b, s, c: (b, 0, 0))] * 2
        args += [h0f, h0b]
    out_specs = [pl.BlockSpec((CHUNK, D_MODEL), lambda b, s, c: (b * nc + s * c, 0))]
    out_shape = [jax.ShapeDtypeStruct((n_tok, D_MODEL), F32)]
    if emit_state:
        out_specs += [pl.BlockSpec((1, hp, D_STATE), lambda b, s, c: (b, 0, 0))] * 2
        out_shape += [jax.ShapeDtypeStruct((n_batch, hp, D_STATE), F32)] * 2
    kern = functools.partial(_ssd_scan_kernel, n_chunks=nc, has_h0=has_h0, emit_state=emit_state)
    return pl.pallas_call(
        kern,
        grid=(n_batch, 2, nc),
        in_specs=in_specs,
        out_specs=out_specs,
        out_shape=out_shape,
        scratch_shapes=[
            pltpu.VMEM((nc, CHUNK, SSD_INNER), F32),
            pltpu.VMEM((D_STATE, hp), F32),
            pltpu.VMEM((D_STATE, hp), F32),
            pltpu.VMEM((CHUNK, SSD_INNER), F32),
            pltpu.VMEM((CHUNK, SSD_INNER), BF16),
        ],
        compiler_params=pltpu.CompilerParams(dimension_semantics=("arbitrary", "arbitrary", "arbitrary"),
                                             vmem_limit_bytes=VMEM_LIMIT),
        name="ssd_scan_L%d" % seq_len,
    )(*args)


def kernel(x_prompt, x_sample, state_fwd, state_bwd, c, c_ctx, ada_w, ada_b, norm_g, conv_w_in, conv_b_in, conv_w_dw, conv_b_dw, conv_ln_g, conv_ln_b, conv_w_out, ssd_w_in, ssd_w_conv, ssd_b_conv, ssd_dt_bias_f, ssd_dt_bias_b, ssd_a_log_f, ssd_a_log_b, ssd_d, ssd_norm_g, ssd_w_out, final_norm_g):
    nb_p, seq_p, _ = x_prompt.shape
    nb_s, seq_s, _ = x_sample.shape
    grid_w = 64

    cond = jnp.concatenate([c_ctx[None, :], c, jnp.zeros((16 - 1 - nb_s, D_MODEL), F32)], axis=0)
    mods4 = _adaln(cond, ada_w, ada_b).reshape(ada_w.shape[0], 16, 3, D_MODEL)

    xp = x_prompt.reshape(nb_p * seq_p, D_MODEL)
    xs = x_sample.reshape(nb_s * seq_s, D_MODEL)

    cw = 256
    nch = CONV_E // cw
    win3 = conv_w_in[0].reshape(D_MODEL, 3, nch, cw).transpose(1, 2, 0, 3).astype(BF16)
    bin3 = conv_b_in[0].reshape(3, nch, 1, cw)
    wdw_s = jnp.pad(conv_w_dw[0], ((0, 32 - CONV_K), (0, 0))).reshape(32, N_SLABS, LANES).transpose(1, 0, 2)
    bdw_s = conv_b_dw[0].reshape(N_SLABS, 1, LANES)
    lng = conv_ln_g[0].reshape(1, CONV_E)
    lnb = conv_ln_b[0].reshape(1, CONV_E)
    wout0 = conv_w_out[0].astype(BF16)
    g0 = norm_g[0].reshape(1, D_MODEL)
    tm = 512
    xp = _conformer_layer(xp, mods4, 0, lambda i: 0, seq_p, g0, win3, bin3, wdw_s, bdw_s, lng, lnb, wout0)
    xs = _conformer_layer(xs, mods4, 0, lambda i: 1 + i // (seq_s // tm), grid_w, g0, win3, bin3, wdw_s, bdw_s,
                          lng, lnb, wout0)

    w_in = ssd_w_in[0]
    wz = w_in[:, :SSD_INNER].astype(BF16)
    wx = w_in[:, SSD_INNER:SSD_INNER + XBC].astype(BF16)
    wd = jnp.pad(w_in[:, SSD_INNER + XBC:], ((0, 0), (0, LANES - 2 * SSD_HEADS))).astype(BF16)
    g1 = norm_g[1].reshape(1, D_MODEL)
    wconv_s = jnp.pad(ssd_w_conv[0], ((0, SUBLANES - SSD_CONV_K), (0, 0))).reshape(
        SUBLANES, XBC_SLABS, LANES).transpose(1, 0, 2)
    bconv_s = ssd_b_conv[0].reshape(XBC_SLABS, 1, LANES)
    zpad = jnp.zeros((LANES - 2 * SSD_HEADS,), F32)
    dtb_row = jnp.concatenate([ssd_dt_bias_f[0], ssd_dt_bias_b[0], zpad]).reshape(1, LANES)
    a_row = jnp.concatenate([-jnp.exp(ssd_a_log_f[0]), -jnp.exp(ssd_a_log_b[0]), zpad]).reshape(1, LANES)
    d_exp = jnp.repeat(ssd_d[0], SSD_HEAD_DIM).reshape(1, SSD_INNER)
    ng_row = ssd_norm_g[0].reshape(1, SSD_INNER)
    wout1 = ssd_w_out[0].astype(BF16)
    fin_g = final_norm_g.reshape(1, D_MODEL)
    hp = SSD_HEADS * SSD_HEAD_DIM

    zp, actp, dtcp = _ssd_proj(xp, mods4, 1, lambda i: 0, seq_p, g1, wz, wx, wd, wconv_s, bconv_s, dtb_row, a_row)
    zs_, acts, dtcs = _ssd_proj(xs, mods4, 1, lambda i: 1 + i // (seq_s // tm), seq_s, g1, wz, wx, wd, wconv_s,
                                bconv_s, dtb_row, a_row)

    yp, sf, sb = _ssd_scan(xp, zp, actp, dtcp, mods4, 1, lambda b: 0, seq_p, fin_g, d_exp, ng_row, wout1,
                           None, None, True)
    (ys,) = _ssd_scan(xs, zs_, acts, dtcs, mods4, 1, lambda b: 1 + b, seq_s, fin_g, d_exp, ng_row, wout1,
                      state_fwd[:, 0].reshape(nb_s, hp, D_STATE), state_bwd[:, 0].reshape(nb_s, hp, D_STATE), False)

    y_prompt = yp.reshape(nb_p, seq_p, D_MODEL)
    y_sample = ys.reshape(nb_s, seq_s, D_MODEL)
    new_f = sf.reshape(nb_p, 1, SSD_HEADS, SSD_HEAD_DIM, D_STATE)
    new_b = sb.reshape(nb_p, 1, SSD_HEADS, SSD_HEAD_DIM, D_STATE)
    return (y_prompt, y_sample, new_f, new_b)
```

```python
import functools

import jax
import jax.numpy as jnp
from jax import lax
from jax.experimental import pallas as pl
from jax.experimental.pallas import tpu as pltpu

F32 = jnp.float32
BF16 = jnp.bfloat16

D_MODEL = 1024
CONV_E = 2048
CONV_K = 31
CONV_HALF = CONV_K // 2
SSD_INNER = 2048
SSD_HEADS = 32
SSD_HEAD_DIM = 64
SSD_GROUPS = 4
D_STATE = 128
SSD_CONV_K = 5
SSD_CONV_HALF = SSD_CONV_K // 2
XBC = SSD_INNER + 2 * SSD_GROUPS * D_STATE
CHUNK = 128
EPS = 1e-6

LANES = 128
SUBLANES = 8
N_SLABS = CONV_E // LANES
XBC_SLABS = XBC // LANES
X_SLABS = SSD_INNER // LANES
B_SLAB0 = X_SLABS
C_SLAB0 = X_SLABS + SSD_GROUPS
PAD = 16
HALO = 16
ROW_CHUNK = 64
VMEM_LIMIT = 56 * 1024 * 1024


def _sigmoid(x):
    return jax.nn.sigmoid(x)


def _silu(x):
    return x * _sigmoid(x)


def _softplus(x):
    return jnp.maximum(x, 0.0) + jnp.log1p(jnp.exp(-jnp.abs(x)))


def _dot(a, b):
    return jnp.dot(a, b, preferred_element_type=F32)


def _adaln_kernel(cond_ref, w_ref, b_ref, o_ref):
    s = _silu(cond_ref[...])
    o_ref[0] = _dot(s.astype(BF16), w_ref[0].astype(BF16)) + b_ref[0]


def _adaln(cond16, ada_w, ada_b):
    depth = ada_w.shape[0]
    nt = 4
    tn = 3 * D_MODEL // nt
    return pl.pallas_call(
        _adaln_kernel,
        grid=(depth, nt),
        in_specs=[
            pl.BlockSpec((16, D_MODEL), lambda i, j: (0, 0)),
            pl.BlockSpec((1, D_MODEL, tn), lambda i, j: (i, 0, j)),
            pl.BlockSpec((1, 1, tn), lambda i, j: (i, 0, j)),
        ],
        out_specs=pl.BlockSpec((1, 16, tn), lambda i, j: (i, 0, j)),
        out_shape=jax.ShapeDtypeStruct((depth, 16, 3 * D_MODEL), F32),
        name="adaln",
    )(cond16, ada_w, ada_b.reshape(depth, 1, 3 * D_MODEL))


def _modulated_norm(x, g, sh, sc):
    ms = jnp.mean(x * x, axis=-1, keepdims=True)
    return (x * lax.rsqrt(ms + EPS) * g) * (1.0 + sc) + sh


def _modulated_norm_to(h_ref, x_ref, g_ref, mod_ref, tm):
    g = g_ref[...]
    sh = mod_ref[0, 0, 0:1, :]
    sc = mod_ref[0, 0, 1:2, :]

    def body(r, carry):
        rows = pl.ds(pl.multiple_of(r * 32, 32), 32)
        h_ref[rows, :] = _modulated_norm(x_ref[rows, :], g, sh, sc).astype(BF16)
        return carry

    lax.fori_loop(0, tm // 32, body, 0, unroll=4)


def _conv_geometry(seg_len, tm):
    n_seg = tm // seg_len
    if seg_len == 64:
        assert n_seg == SUBLANES
        pitch, stride, n_grp, grp_pitch, n_t0, t_unroll = 84, 84, 1, 0, seg_len, 16
    else:
        assert seg_len == 256
        pitch, stride, n_grp, grp_pitch, n_t0, t_unroll = 304, 36, n_seg, 304, 36, 12
    rows = PAD + n_seg * pitch
    rows = -(-(rows + PAD) // SUBLANES) * SUBLANES if seg_len == 64 else rows
    pieces = []
    for p in range(tm // ROW_CHUNK):
        t = p * ROW_CHUNK
        pieces.append((t, PAD + pitch * (t // seg_len) + (t % seg_len)))
    assert n_t0 % t_unroll == 0
    return dict(pitch=pitch, stride=stride, n_grp=n_grp, grp_pitch=grp_pitch, n_t0=n_t0, rows=rows,
                t_unroll=t_unroll, pieces=tuple(pieces))


def _conformer_kernel(x_ref, mod_ref, g_ref, win_ref, bin_ref, wdw_ref, bdw_ref, lng_ref, lnb_ref, wout_ref,
                      o_ref, h_ref, cina_ref, cinb_ref, cout_ref, zs_ref, m_ref, *, tm, geo, cw, mb):
    nch = CONV_E // cw
    spc = cw // LANES
    stride = geo["stride"]
    tb = geo["t_unroll"]
    assert tm // mb == spc and nch % 2 == 0

    @pl.when(pl.program_id(0) == 0)
    def _():
        cina_ref[...] = jnp.zeros(cina_ref.shape, F32)
        cinb_ref[...] = jnp.zeros(cinb_ref.shape, F32)

    _modulated_norm_to(h_ref, x_ref, g_ref, mod_ref, tm)

    def proj_part(n, m, cin_ref):
        hm = h_ref[m * mb:(m + 1) * mb, :]
        a = _dot(hm, win_ref[0, n]) + bin_ref[0, n]
        gt = _dot(hm, win_ref[1, n]) + bin_ref[1, n]
        v = a * _sigmoid(gt)
        for q in range(spc):
            for (t, crow) in geo["pieces"]:
                if m * mb <= t < (m + 1) * mb:
                    cin_ref[q, pl.ds(crow, ROW_CHUNK), :] = (
                        v[t - m * mb:t - m * mb + ROW_CHUNK, q * LANES:(q + 1) * LANES])
        z = _dot(hm, win_ref[2, n]) + bin_ref[2, n]
        zs_ref[n, m * mb:(m + 1) * mb, :] = _silu(z)

    def conv_part(n, q, cin_ref):
        sl = n * spc + q
        wk = [jnp.broadcast_to(wdw_ref[sl, k:k + 1, :], (SUBLANES, LANES)) for k in range(CONV_K)]
        bias = jnp.broadcast_to(bdw_ref[sl], (SUBLANES, LANES))
        for grp in range(geo["n_grp"]):
            for ti in range(geo["n_t0"] // tb):
                base = PAD + grp * geo["grp_pitch"] + ti * tb
                accs = [bias] * tb
                for j in range(tb + CONV_K - 1):
                    xrow = cin_ref[q, pl.ds(base + (j - CONV_HALF), SUBLANES, stride=stride), :]
                    for u in range(max(0, j - CONV_K + 1), min(tb, j + 1)):
                        accs[u] = accs[u] + wk[j - u] * xrow
                for u in range(tb):
                    cout_ref[sl, pl.ds(base + u, SUBLANES, stride=stride), :] = accs[u]

    def proj_and_conv(n_proj, cin_proj, n_conv, cin_conv):
        for part in range(spc):
            proj_part(n_proj, part, cin_proj)
            conv_part(n_conv, part, cin_conv)

    for m in range(spc):
        proj_part(0, m, cina_ref)

    def pair_body(i, carry):
        n = 2 * i + 1
        proj_and_conv(n, cinb_ref, n - 1, cina_ref)
        proj_and_conv(n + 1, cina_ref, n, cinb_ref)
        return carry

    lax.fori_loop(0, nch // 2 - 1, pair_body, 0)
    proj_and_conv(nch - 1, cinb_ref, nch - 2, cina_ref)
    for q in range(spc):
        conv_part(nch - 1, q, cinb_ref)

    for (t, crow) in geo["pieces"]:
        rows = pl.ds(crow, ROW_CHUNK)
        s1 = cout_ref[0, rows, :]
        for sl in range(1, N_SLABS):
            s1 = s1 + cout_ref[sl, rows, :]
        mu = jnp.sum(s1, axis=-1, keepdims=True) * (1.0 / CONV_E)
        s2 = jnp.zeros((ROW_CHUNK, LANES), F32)
        for sl in range(N_SLABS):
            d = cout_ref[sl, rows, :] - mu
            s2 = s2 + d * d
        var = jnp.sum(s2, axis=-1, keepdims=True) * (1.0 / CONV_E)
        rstd = lax.rsqrt(var + EPS)
        for sl in range(N_SLABS):
            cs = slice(sl * LANES, (sl + 1) * LANES)
            yn = (cout_ref[sl, rows, :] - mu) * rstd * lng_ref[:, cs] + lnb_ref[:, cs]
            zq = zs_ref[sl * LANES // cw, t:t + ROW_CHUNK, (sl * LANES) % cw:(sl * LANES) % cw + LANES]
            m_ref[t:t + ROW_CHUNK, cs] = (_silu(yn) * zq).astype(BF16)

    gate = mod_ref[0, 0, 2:3, :]
    for m in range(tm // mb):
        rows = slice(m * mb, (m + 1) * mb)
        out = _dot(m_ref[rows, :], wout_ref[...])
        o_ref[rows, :] = x_ref[rows, :] + gate * out


def _conformer_layer(x2d, mods4, layer, cond_of_block, seg_len, g_row, win3, bin3, wdw_s, bdw_s, lng, lnb, wout):
    n_tok = x2d.shape[0]
    tm = 512
    cw = 256
    mb = 256
    geo = _conv_geometry(seg_len, tm)
    nch = CONV_E // cw
    const = dict(pipeline_mode=pl.Buffered(1))
    kern = functools.partial(_conformer_kernel, tm=tm, geo=geo, cw=cw, mb=mb)
    return pl.pallas_call(
        kern,
        grid=(n_tok // tm,),
        in_specs=[
            pl.BlockSpec((tm, D_MODEL), lambda i: (i, 0)),
            pl.BlockSpec((1, 1, 3, D_MODEL), lambda i: (layer, cond_of_block(i), 0, 0)),
            pl.BlockSpec((1, D_MODEL), lambda i: (0, 0)),
            pl.BlockSpec((3, nch, D_MODEL, cw), lambda i: (0, 0, 0, 0), **const),
            pl.BlockSpec((3, nch, 1, cw), lambda i: (0, 0, 0, 0)),
            pl.BlockSpec((N_SLABS, 32, LANES), lambda i: (0, 0, 0)),
            pl.BlockSpec((N_SLABS, 1, LANES), lambda i: (0, 0, 0)),
            pl.BlockSpec((1, CONV_E), lambda i: (0, 0)),
            pl.BlockSpec((1, CONV_E), lambda i: (0, 0)),
            pl.BlockSpec((CONV_E, D_MODEL), lambda i: (0, 0), **const),
        ],
        out_specs=pl.BlockSpec((tm, D_MODEL), lambda i: (i, 0)),
        out_shape=jax.ShapeDtypeStruct((n_tok, D_MODEL), F32),
        scratch_shapes=[
            pltpu.VMEM((tm, D_MODEL), BF16),
            pltpu.VMEM((cw // LANES, geo["rows"], LANES), F32),
            pltpu.VMEM((cw // LANES, geo["rows"], LANES), F32),
            pltpu.VMEM((N_SLABS, geo["rows"], LANES), F32),
            pltpu.VMEM((nch, tm, cw), F32),
            pltpu.VMEM((tm, CONV_E), BF16),
        ],
        compiler_params=pltpu.CompilerParams(dimension_semantics=("arbitrary",), vmem_limit_bytes=VMEM_LIMIT),
        name="conformer_seg%d" % seg_len,
    )(x2d, mods4, g_row, win3, bin3, wdw_s, bdw_s, lng, lnb, wout)


def _split3(x):
    hi = x.astype(BF16)
    r1 = x - hi.astype(F32)
    mid = r1.astype(BF16)
    lo = (r1 - mid.astype(F32)).astype(BF16)
    return hi, mid, lo


def _ssd_proj_kernel(x_ref, xp_ref, xn_ref, mod_ref, g_ref, wz_ref, wx_ref, wd_ref, wconv_ref, bconv_ref,
                     dtb_ref, arow_ref, z_ref, act_ref, dtc_ref, h_ref, exta_ref, extb_ref, asa_ref, asb_ref,
                     *, tm, blocks_per_seq, nw):
    i = pl.program_id(0)
    stride = (tm + 2 * HALO) // SUBLANES
    row0 = SUBLANES + HALO

    @pl.when(i == 0)
    def _():
        exta_ref[...] = jnp.zeros(exta_ref.shape, F32)
        extb_ref[...] = jnp.zeros(extb_ref.shape, F32)

    _modulated_norm_to(h_ref, x_ref, g_ref, mod_ref, tm)
    g = g_ref[...]
    sh = mod_ref[0, 0, 0:1, :]
    sc = mod_ref[0, 0, 1:2, :]
    pos = i % blocks_per_seq
    hp = jnp.where(pos > 0, _modulated_norm(xp_ref[...], g, sh, sc), 0.0)
    hn = jnp.where(pos < blocks_per_seq - 1, _modulated_norm(xn_ref[...], g, sh, sc), 0.0)
    h_ref[tm:tm + HALO, :] = hp.astype(BF16)
    h_ref[tm + HALO:tm + 2 * HALO, :] = hn.astype(BF16)

    spc = nw // LANES
    n_chunks = XBC // nw
    assert n_chunks % 2 == 0
    m_chunks = [(m0, min(m0 + 256, tm)) for m0 in range(0, tm, 256)]
    m_chunks[-1] = (m_chunks[-1][0], tm + 2 * HALO)
    n_trips = 4
    tb = stride // n_trips

    def xproj(j, ext_ref):
        for (m0, m1) in m_chunks:
            nb = min(m1, tm) - m0
            r = _dot(h_ref[m0:m1, :], wx_ref[j])
            for q in range(spc):
                ls = slice(q * LANES, (q + 1) * LANES)
                ext_ref[q, row0 + m0:row0 + m0 + nb, :] = r[0:nb, ls]
                if m1 > tm:
                    ext_ref[q, SUBLANES:SUBLANES + HALO, :] = r[nb:nb + HALO, ls]
                    ext_ref[q, row0 + tm:row0 + tm + HALO, :] = r[nb + HALO:nb + 2 * HALO, ls]

    def conv(j, ext_ref, as_ref):
        for q in range(spc):
            sl = j * spc + q
            wk = [jnp.broadcast_to(wconv_ref[sl, k:k + 1, :], (SUBLANES, LANES)) for k in range(SSD_CONV_K)]
            bias = jnp.broadcast_to(bconv_ref[sl], (SUBLANES, LANES))
            for ti in range(n_trips):
                base = SUBLANES + ti * tb
                accs = [bias] * tb
                for jj in range(tb + SSD_CONV_K - 1):
                    xrow = ext_ref[q, pl.ds(base + (jj - SSD_CONV_HALF), SUBLANES, stride=stride), :]
                    for u in range(max(0, jj - SSD_CONV_K + 1), min(tb, jj + 1)):
                        accs[u] = accs[u] + wk[jj - u] * xrow
                for u in range(tb):
                    as_ref[q, pl.ds(base + u, SUBLANES, stride=stride), :] = _silu(accs[u])
            act_ref[sl] = as_ref[q, row0:row0 + tm, :]

    xproj(0, exta_ref)

    def pair_body(p, carry):
        j = 2 * p + 1
        xproj(j, extb_ref)
        conv(j - 1, exta_ref, asa_ref)
        xproj(j + 1, exta_ref)
        conv(j, extb_ref, asb_ref)
        return carry

    lax.fori_loop(0, n_chunks // 2 - 1, pair_body, 0)
    xproj(n_chunks - 1, extb_ref)
    conv(n_chunks - 2, exta_ref, asa_ref)

    conv(n_chunks - 1, extb_ref, asb_ref)
    for m in range(tm // 256):
        rows = slice(m * 256, (m + 1) * 256)
        hm = h_ref[rows, :]
        for j in range(SSD_INNER // nw):
            cs = slice(j * nw, (j + 1) * nw)
            z_ref[rows, cs] = _dot(hm, wz_ref[:, cs])

    t = CHUNK
    row = lax.broadcasted_iota(jnp.int32, (t, t), 0)
    col = lax.broadcasted_iota(jnp.int32, (t, t), 1)
    lmat = jnp.where(col <= row, 1.0, 0.0).astype(BF16)
    umat = jnp.where(col >= row, 1.0, 0.0).astype(BF16)
    for j in range(tm // t):
        rows = slice(j * t, (j + 1) * t)
        dt_all = _softplus(_dot(h_ref[rows, :], wd_ref[...]) + dtb_ref[...])
        parts = _split3(dt_all * arow_ref[...])
        cum_f = _dot(lmat, parts[0]) + _dot(lmat, parts[1]) + _dot(lmat, parts[2])
        cum_b = _dot(umat, parts[0]) + _dot(umat, parts[1]) + _dot(umat, parts[2])
        cum = jnp.where(col < SSD_HEADS, cum_f, cum_b)
        cum_t = cum.T
        dt_t = dt_all.T
        edge = jnp.where(row < SSD_HEADS, cum_t[:, t - 1:t], cum_t[:, 0:1])
        dtc_ref[0, rows, :] = cum
        dtc_ref[1, rows, :] = cum_t
        dtc_ref[2, rows, :] = dt_t
        dtc_ref[3, rows, :] = dt_t * jnp.exp(edge - cum_t)


def _ssd_proj(x2d, mods4, layer, cond_of_block, seq_len, g_row, wz, wx, wd, wconv_s, bconv_s, dtb_row, a_row):
    n_tok = x2d.shape[0]
    tm = min(512, seq_len)
    bps = seq_len // tm
    hb = tm // HALO
    last_hb = n_tok // HALO - 1
    rows = tm + 2 * HALO + 2 * SUBLANES
    const = dict(pipeline_mode=pl.Buffered(1))
    nw = wx.shape[-1]
    kern = functools.partial(_ssd_proj_kernel, tm=tm, blocks_per_seq=bps, nw=nw)
    return pl.pallas_call(
        kern,
        grid=(n_tok // tm,),
        in_specs=[
            pl.BlockSpec((tm, D_MODEL), lambda i: (i, 0)),
            pl.BlockSpec((HALO, D_MODEL), lambda i: (jnp.maximum(i * hb - 1, 0), 0)),
            pl.BlockSpec((HALO, D_MODEL), lambda i: (jnp.minimum((i + 1) * hb, last_hb), 0)),
            pl.BlockSpec((1, 1, 3, D_MODEL), lambda i: (layer, cond_of_block(i), 0, 0)),
            pl.BlockSpec((1, D_MODEL), lambda i: (0, 0)),
            pl.BlockSpec((D_MODEL, SSD_INNER), lambda i: (0, 0), **const),
            pl.BlockSpec((XBC // nw, D_MODEL, nw), lambda i: (0, 0, 0), **const),
            pl.BlockSpec((D_MODEL, LANES), lambda i: (0, 0), **const),
            pl.BlockSpec((XBC_SLABS, SUBLANES, LANES), lambda i: (0, 0, 0)),
            pl.BlockSpec((XBC_SLABS, 1, LANES), lambda i: (0, 0, 0)),
            pl.BlockSpec((1, LANES), lambda i: (0, 0)),
            pl.BlockSpec((1, LANES), lambda i: (0, 0)),
        ],
        out_specs=[
            pl.BlockSpec((tm, SSD_INNER), lambda i: (i, 0)),
            pl.BlockSpec((XBC_SLABS, tm, LANES), lambda i: (0, i, 0)),
            pl.BlockSpec((4, tm, LANES), lambda i: (0, i, 0)),
        ],
        out_shape=[
            jax.ShapeDtypeStruct((n_tok, SSD_INNER), F32),
            jax.ShapeDtypeStruct((XBC_SLABS, n_tok, LANES), F32),
            jax.ShapeDtypeStruct((4, n_tok, LANES), F32),
        ],
        scratch_shapes=[
            pltpu.VMEM((tm + 2 * HALO, D_MODEL), BF16),
            pltpu.VMEM((nw // LANES, rows, LANES), F32),
            pltpu.VMEM((nw // LANES, rows, LANES), F32),
            pltpu.VMEM((nw // LANES, rows, LANES), F32),
            pltpu.VMEM((nw // LANES, rows, LANES), F32),
        ],
        compiler_params=pltpu.CompilerParams(dimension_semantics=("arbitrary",), vmem_limit_bytes=VMEM_LIMIT),
        name="ssd_proj_L%d" % seq_len,
    )(x2d, x2d, x2d, mods4, g_row, wz, wx, wd, wconv_s, bconv_s, dtb_row, a_row)


def _ssd_scan_kernel(*refs, n_steps, sub_chunks, has_h0, emit_state):
    it = iter(refs)
    act_ref, dtc_ref, z_ref, x_ref, mod_ref, fing_ref, dexp_ref, ng_ref, wout_ref = (next(it) for _ in range(9))
    if has_h0:
        h0f_ref, h0b_ref = next(it), next(it)
    o_ref = next(it)
    if emit_state:
        sf_ref, sb_ref = next(it), next(it)
    yb_ref, ht_ref, gt_ref, y_ref, m_ref = (next(it) for _ in range(5))

    t = CHUNK
    s = pl.program_id(1)
    c = pl.program_id(2)
    bi = s * c + (1 - s) * (n_steps - 1 - c)

    row = lax.broadcasted_iota(jnp.int32, (t, t), 0)
    col = lax.broadcasted_iota(jnp.int32, (t, t), 1)
    lower = col <= row
    upper = col >= row
    lane_lo = col < SSD_HEAD_DIM
    cum_k, cumt_k, dtt_k, w1t_k = 0, 1, 2, 3

    def colb(r0, j):
        return jnp.broadcast_to(dtc_ref[cum_k, r0:r0 + t, j:j + 1], (t, LANES))

    def rowb(r0, k, j):
        return jnp.broadcast_to(dtc_ref[k, r0 + j:r0 + j + 1, :], (t, t))

    def pair_rhs(xs_pair):
        return jnp.concatenate([jnp.where(lane_lo, xs_pair, 0.0).astype(BF16),
                                jnp.where(lane_lo, 0.0, xs_pair).astype(BF16)], axis=0)

    def state_lhs(b_t, r0, j0):
        return jnp.concatenate([(b_t * rowb(r0, w1t_k, j0)).astype(BF16),
                                (b_t * rowb(r0, w1t_k, j0 + 1)).astype(BF16)], axis=1)

    hp_per_g = SSD_HEADS // SSD_GROUPS // 2

    def bwd_chunk(sub):
        r0 = sub * t
        rows = slice(r0, r0 + t)
        ci = bi * sub_chunks + sub
        for g in range(SSD_GROUPS):
            gs = slice(g * 512, (g + 1) * 512)
            c_g = act_ref[C_SLAB0 + g, rows, :].astype(BF16)
            b_t = act_ref[B_SLAB0 + g, rows, :].T
            yoff = _dot(c_g, gt_ref[:, gs].astype(BF16))
            for q in range(hp_per_g):
                hp = g * hp_per_g + q
                ps = slice(hp * LANES, (hp + 1) * LANES)
                j0 = SSD_HEADS + 2 * hp
                r_pair = jnp.where(lane_lo, colb(r0, j0), colb(r0, j0 + 1))
                yb_ref[ci, :, ps] = yoff[:, q * LANES:(q + 1) * LANES] * jnp.exp(r_pair)
                upd = _dot(state_lhs(b_t, r0, j0), pair_rhs(act_ref[hp, rows, :]))
                gt_ref[:, ps] = gt_ref[:, ps] * jnp.exp(r_pair[0:1, :]) + upd

    def fwd_chunk(sub):
        r0 = sub * t
        rows = slice(r0, r0 + t)
        ci = bi * sub_chunks + sub
        for g in range(SSD_GROUPS):
            gs = slice(g * 512, (g + 1) * 512)
            b_f32 = act_ref[B_SLAB0 + g, rows, :]
            b_t = b_f32.T
            c_g = act_ref[C_SLAB0 + g, rows, :].astype(BF16)
            cb = lax.dot_general(c_g, b_f32.astype(BF16), (((1,), (1,)), ((), ())), preferred_element_type=F32)
            y_ref[rows, gs] = _dot(c_g, ht_ref[:, gs].astype(BF16))
            for q in range(hp_per_g):
                hp = g * hp_per_g + q
                ps = slice(hp * LANES, (hp + 1) * LANES)
                lhs = []
                col_f = []
                for e in range(2):
                    h = 2 * hp + e
                    hb = SSD_HEADS + h
                    cf = colb(r0, h)
                    col_f.append(cf)
                    dec_f = jnp.where(lower, jnp.exp(cf - rowb(r0, cumt_k, h)), 0.0) * rowb(r0, dtt_k, h)
                    dec_b = (jnp.where(upper, jnp.exp(colb(r0, hb) - rowb(r0, cumt_k, hb)), 0.0)
                             * rowb(r0, dtt_k, hb))
                    lhs.append((cb * (dec_f + dec_b)).astype(BF16))
                xs_pair = act_ref[hp, rows, :]
                rhs = pair_rhs(xs_pair)
                y_diag = _dot(jnp.concatenate(lhs, axis=1), rhs)
                cf_pair = jnp.where(lane_lo, col_f[0], col_f[1])
                y_ref[rows, ps] = (y_diag + y_ref[rows, ps] * jnp.exp(cf_pair)
                                   + yb_ref[ci, :, ps] + dexp_ref[:, ps] * xs_pair)
                upd = _dot(state_lhs(b_t, r0, 2 * hp), rhs)
                ht_ref[:, ps] = ht_ref[:, ps] * jnp.exp(cf_pair[t - 1:t, :]) + upd

        gw = SSD_INNER // SSD_GROUPS
        for k in range(SSD_GROUPS):
            ks = slice(k * gw, (k + 1) * gw)
            yg = y_ref[rows, ks] * _silu(z_ref[rows, ks])
            ms = jnp.mean(yg * yg, axis=-1, keepdims=True)
            m_ref[rows, ks] = (yg * lax.rsqrt(ms + EPS) * ng_ref[:, ks]).astype(BF16)
        out = _dot(m_ref[rows, :], wout_ref[...])
        ynew = x_ref[rows, :] + mod_ref[0, 0, 2:3, :] * out
        ms = jnp.mean(ynew * ynew, axis=-1, keepdims=True)
        o_ref[rows, :] = ynew * lax.rsqrt(ms + EPS) * fing_ref[...]

    @pl.when(s == 0)
    def _():
        @pl.when(c == 0)
        def _():
            if has_h0:
                gt_ref[...] = h0b_ref[0].T
            else:
                gt_ref[...] = jnp.zeros(gt_ref.shape, F32)

        for sub in reversed(range(sub_chunks)):
            bwd_chunk(sub)

        if emit_state:
            @pl.when(c == n_steps - 1)
            def _():
                sb_ref[0] = gt_ref[...].T

    @pl.when(s == 1)
    def _():
        @pl.when(c == 0)
        def _():
            if has_h0:
                ht_ref[...] = h0f_ref[0].T
            else:
                ht_ref[...] = jnp.zeros(ht_ref.shape, F32)

        for sub in range(sub_chunks):
            fwd_chunk(sub)

        if emit_state:
            @pl.when(c == n_steps - 1)
            def _():
                sf_ref[0] = ht_ref[...].T


def _ssd_scan(x2d, z, act, dtc, mods4, layer, cond_of_batch, seq_len, fin_g, d_exp, ng_row, wout, h0f, h0b,
              emit_state):
    n_tok = x2d.shape[0]
    n_batch = n_tok // seq_len
    nc = seq_len // CHUNK
    sub = 2
    ns = nc // sub
    tr = sub * CHUNK
    has_h0 = h0f is not None

    def blk(b, s, c):
        return b * ns + s * c + (1 - s) * (ns - 1 - c)

    const = dict(pipeline_mode=pl.Buffered(1))
    hp = SSD_HEADS * SSD_HEAD_DIM
    in_specs = [
        pl.BlockSpec((XBC_SLABS, tr, LANES), lambda b, s, c: (0, blk(b, s, c), 0)),
        pl.BlockSpec((4, tr, LANES), lambda b, s, c: (0, blk(b, s, c), 0)),
        pl.BlockSpec((tr, SSD_INNER), lambda b, s, c: (b * ns + s * c, 0)),
        pl.BlockSpec((tr, D_MODEL), lambda b, s, c: (b * ns + s * c, 0)),
        pl.BlockSpec((1, 1, 3, D_MODEL), lambda b, s, c: (layer, cond_of_batch(b), 0, 0)),
        pl.BlockSpec((1, D_MODEL), lambda b, s, c: (0, 0)),
        pl.BlockSpec((1, SSD_INNER), lambda b, s, c: (0, 0)),
        pl.BlockSpec((1, SSD_INNER), lambda b, s, c: (0, 0)),
        pl.BlockSpec((SSD_INNER, D_MODEL), lambda b, s, c: (0, 0), **const),
    ]
    args = [act, dtc, z, x2d, mods4, fin_g, d_exp, ng_row, wout]
    if has_h0:
        in_specs += [pl.BlockSpec((1, hp, D_STATE), lambda b, s, c: (b, 0, 0))] * 2
        args += [h0f, h0b]
    out_specs = [pl.BlockSpec((tr, D_MODEL), lambda b, s, c: (b * ns + s * c, 0))]
    out_shape = [jax.ShapeDtypeStruct((n_tok, D_MODEL), F32)]
    if emit_state:
        out_specs += [pl.BlockSpec((1, hp, D_STATE), lambda b, s, c: (b, 0, 0))] * 2
        out_shape += [jax.ShapeDtypeStruct((n_batch, hp, D_STATE), F32)] * 2
    kern = functools.partial(_ssd_scan_kernel, n_steps=ns, sub_chunks=sub, has_h0=has_h0, emit_state=emit_state)
    return pl.pallas_call(
        kern,
        grid=(n_batch, 2, ns),
        in_specs=in_specs,
        out_specs=out_specs,
        out_shape=out_shape,
        scratch_shapes=[
            pltpu.VMEM((nc, CHUNK, SSD_INNER), F32),
            pltpu.VMEM((D_STATE, hp), F32),
            pltpu.VMEM((D_STATE, hp), F32),
            pltpu.VMEM((tr, SSD_INNER), F32),
            pltpu.VMEM((tr, SSD_INNER), BF16),
        ],
        compiler_params=pltpu.CompilerParams(dimension_semantics=("arbitrary", "arbitrary", "arbitrary"),
                                             vmem_limit_bytes=VMEM_LIMIT),
        name="ssd_scan_L%d" % seq_len,
    )(*args)


def kernel(x_prompt, x_sample, state_fwd, state_bwd, c, c_ctx, ada_w, ada_b, norm_g, conv_w_in, conv_b_in, conv_w_dw, conv_b_dw, conv_ln_g, conv_ln_b, conv_w_out, ssd_w_in, ssd_w_conv, ssd_b_conv, ssd_dt_bias_f, ssd_dt_bias_b, ssd_a_log_f, ssd_a_log_b, ssd_d, ssd_norm_g, ssd_w_out, final_norm_g):
    nb_p, seq_p, _ = x_prompt.shape
    nb_s, seq_s, _ = x_sample.shape
    grid_w = 64

    cond = jnp.concatenate([c_ctx[None, :], c, jnp.zeros((16 - 1 - nb_s, D_MODEL), F32)], axis=0)
    mods4 = _adaln(cond, ada_w, ada_b).reshape(ada_w.shape[0], 16, 3, D_MODEL)

    xp = x_prompt.reshape(nb_p * seq_p, D_MODEL)
    xs = x_sample.reshape(nb_s * seq_s, D_MODEL)

    cw = 256
    nch = CONV_E // cw
    win3 = conv_w_in[0].reshape(D_MODEL, 3, nch, cw).transpose(1, 2, 0, 3).astype(BF16)
    bin3 = conv_b_in[0].reshape(3, nch, 1, cw)
    wdw_s = jnp.pad(conv_w_dw[0], ((0, 32 - CONV_K), (0, 0))).reshape(32, N_SLABS, LANES).transpose(1, 0, 2)
    bdw_s = conv_b_dw[0].reshape(N_SLABS, 1, LANES)
    lng = conv_ln_g[0].reshape(1, CONV_E)
    lnb = conv_ln_b[0].reshape(1, CONV_E)
    wout0 = conv_w_out[0].astype(BF16)
    g0 = norm_g[0].reshape(1, D_MODEL)
    tm = 512
    xp = _conformer_layer(xp, mods4, 0, lambda i: 0, seq_p, g0, win3, bin3, wdw_s, bdw_s, lng, lnb, wout0)
    xs = _conformer_layer(xs, mods4, 0, lambda i: 1 + i // (seq_s // tm), grid_w, g0, win3, bin3, wdw_s, bdw_s,
                          lng, lnb, wout0)

    w_in = ssd_w_in[0]
    wz = w_in[:, :SSD_INNER].astype(BF16)
    nw = 512
    wx = w_in[:, SSD_INNER:SSD_INNER + XBC].reshape(D_MODEL, XBC // nw, nw).transpose(1, 0, 2).astype(BF16)
    wd = jnp.pad(w_in[:, SSD_INNER + XBC:], ((0, 0), (0, LANES - 2 * SSD_HEADS))).astype(BF16)
    g1 = norm_g[1].reshape(1, D_MODEL)
    wconv_s = jnp.pad(ssd_w_conv[0], ((0, SUBLANES - SSD_CONV_K), (0, 0))).reshape(
        SUBLANES, XBC_SLABS, LANES).transpose(1, 0, 2)
    bconv_s = ssd_b_conv[0].reshape(XBC_SLABS, 1, LANES)
    zpad = jnp.zeros((LANES - 2 * SSD_HEADS,), F32)
    dtb_row = jnp.concatenate([ssd_dt_bias_f[0], ssd_dt_bias_b[0], zpad]).reshape(1, LANES)
    a_row = jnp.concatenate([-jnp.exp(ssd_a_log_f[0]), -jnp.exp(ssd_a_log_b[0]), zpad]).reshape(1, LANES)
    d_exp = jnp.repeat(ssd_d[0], SSD_HEAD_DIM).reshape(1, SSD_INNER)
    ng_row = ssd_norm_g[0].reshape(1, SSD_INNER)
    wout1 = ssd_w_out[0].astype(BF16)
    fin_g = final_norm_g.reshape(1, D_MODEL)
    hp = SSD_HEADS * SSD_HEAD_DIM

    zp, actp, dtcp = _ssd_proj(xp, mods4, 1, lambda i: 0, seq_p, g1, wz, wx, wd, wconv_s, bconv_s, dtb_row, a_row)
    zs_, acts, dtcs = _ssd_proj(xs, mods4, 1, lambda i: 1 + i // (seq_s // tm), seq_s, g1, wz, wx, wd, wconv_s,
                                bconv_s, dtb_row, a_row)

    yp, sf, sb = _ssd_scan(xp, zp, actp, dtcp, mods4, 1, lambda b: 0, seq_p, fin_g, d_exp, ng_row, wout1,
                           None, None, True)
    (ys,) = _ssd_scan(xs, zs_, acts, dtcs, mods4, 1, lambda b: 1 + b, seq_s, fin_g, d_exp, ng_row, wout1,
                      state_fwd[:, 0].reshape(nb_s, hp, D_STATE), state_bwd[:, 0].reshape(nb_s, hp, D_STATE), False)

    y_prompt = yp.reshape(nb_p, seq_p, D_MODEL)
    y_sample = ys.reshape(nb_s, seq_s, D_MODEL)
    new_f = sf.reshape(nb_p, 1, SSD_HEADS, SSD_HEAD_DIM, D_STATE)
    new_b = sb.reshape(nb_p, 1, SSD_HEADS, SSD_HEAD_DIM, D_STATE)
    return (y_prompt, y_sample, new_f, new_b)
```

```python
import functools

import jax
import jax.numpy as jnp
from jax import lax
from jax.experimental import pallas as pl
from jax.experimental.pallas import tpu as pltpu

F32 = jnp.float32
BF16 = jnp.bfloat16

D_MODEL = 1024
CONV_E = 2048
CONV_K = 31
CONV_HALF = CONV_K // 2
SSD_INNER = 2048
SSD_HEADS = 32
SSD_HEAD_DIM = 64
SSD_GROUPS = 4
D_STATE = 128
SSD_CONV_K = 5
SSD_CONV_HALF = SSD_CONV_K // 2
XBC = SSD_INNER + 2 * SSD_GROUPS * D_STATE
CHUNK = 128
EPS = 1e-6

LANES = 128
SUBLANES = 8
N_SLABS = CONV_E // LANES
XBC_SLABS = XBC // LANES
X_SLABS = SSD_INNER // LANES
B_SLAB0 = X_SLABS
C_SLAB0 = X_SLABS + SSD_GROUPS
PAD = 16
HALO = 16
ROW_CHUNK = 64
VMEM_LIMIT = 56 * 1024 * 1024


def _sigmoid(x):
    return jax.nn.sigmoid(x)


def _silu(x):
    return x * _sigmoid(x)


def _softplus(x):
    return jnp.maximum(x, 0.0) + jnp.log1p(jnp.exp(-jnp.abs(x)))


def _dot(a, b):
    return jnp.dot(a, b, preferred_element_type=F32)


def _adaln_kernel(cond_ref, w_ref, b_ref, o_ref):
    s = _silu(cond_ref[...])
    o_ref[0] = _dot(s.astype(BF16), w_ref[0].astype(BF16)) + b_ref[0]


def _adaln(cond16, ada_w, ada_b):
    depth = ada_w.shape[0]
    nt = 4
    tn = 3 * D_MODEL // nt
    return pl.pallas_call(
        _adaln_kernel,
        grid=(depth, nt),
        in_specs=[
            pl.BlockSpec((16, D_MODEL), lambda i, j: (0, 0)),
            pl.BlockSpec((1, D_MODEL, tn), lambda i, j: (i, 0, j)),
            pl.BlockSpec((1, 1, tn), lambda i, j: (i, 0, j)),
        ],
        out_specs=pl.BlockSpec((1, 16, tn), lambda i, j: (i, 0, j)),
        out_shape=jax.ShapeDtypeStruct((depth, 16, 3 * D_MODEL), F32),
        name="adaln",
    )(cond16, ada_w, ada_b.reshape(depth, 1, 3 * D_MODEL))


def _modulated_norm(x, g, sh, sc):
    ms = jnp.mean(x * x, axis=-1, keepdims=True)
    return (x * lax.rsqrt(ms + EPS) * g) * (1.0 + sc) + sh


def _modulated_norm_to(h_ref, x_ref, g_ref, mod_ref, tm):
    g = g_ref[...]
    sh = mod_ref[0, 0, 0:1, :]
    sc = mod_ref[0, 0, 1:2, :]

    def body(r, carry):
        rows = pl.ds(pl.multiple_of(r * 32, 32), 32)
        h_ref[rows, :] = _modulated_norm(x_ref[rows, :], g, sh, sc).astype(BF16)
        return carry

    lax.fori_loop(0, tm // 32, body, 0, unroll=4)


def _conv_geometry(seg_len, tm):
    n_seg = tm // seg_len
    if seg_len == 64:
        assert n_seg == SUBLANES
        pitch, stride, n_grp, grp_pitch, n_t0, t_unroll = 84, 84, 1, 0, seg_len, 16
    else:
        assert seg_len == 256
        pitch, stride, n_grp, grp_pitch, n_t0, t_unroll = 304, 36, n_seg, 304, 36, 12
    rows = PAD + n_seg * pitch
    rows = -(-(rows + PAD) // SUBLANES) * SUBLANES if seg_len == 64 else rows
    pieces = []
    for p in range(tm // ROW_CHUNK):
        t = p * ROW_CHUNK
        pieces.append((t, PAD + pitch * (t // seg_len) + (t % seg_len)))
    assert n_t0 % t_unroll == 0
    return dict(pitch=pitch, stride=stride, n_grp=n_grp, grp_pitch=grp_pitch, n_t0=n_t0, rows=rows,
                t_unroll=t_unroll, pieces=tuple(pieces))


def _conformer_kernel(x_ref, mod_ref, g_ref, win_ref, bin_ref, wdw_ref, bdw_ref, lng_ref, lnb_ref, wout_ref,
                      o_ref, h_ref, cina_ref, cinb_ref, cout_ref, zs_ref, m_ref, *, tm, geo, cw, mb):
    nch = CONV_E // cw
    spc = cw // LANES
    stride = geo["stride"]
    tb = geo["t_unroll"]
    assert tm // mb == spc and nch % 2 == 0

    @pl.when(pl.program_id(0) == 0)
    def _():
        cina_ref[...] = jnp.zeros(cina_ref.shape, F32)
        cinb_ref[...] = jnp.zeros(cinb_ref.shape, F32)

    _modulated_norm_to(h_ref, x_ref, g_ref, mod_ref, tm)

    def proj_part(n, m, cin_ref):
        hm = h_ref[m * mb:(m + 1) * mb, :]
        a = _dot(hm, win_ref[0, n]) + bin_ref[0, n]
        gt = _dot(hm, win_ref[1, n]) + bin_ref[1, n]
        v = a * _sigmoid(gt)
        for q in range(spc):
            for (t, crow) in geo["pieces"]:
                if m * mb <= t < (m + 1) * mb:
                    cin_ref[q, pl.ds(crow, ROW_CHUNK), :] = (
                        v[t - m * mb:t - m * mb + ROW_CHUNK, q * LANES:(q + 1) * LANES])
        z = _dot(hm, win_ref[2, n]) + bin_ref[2, n]
        zs_ref[n, m * mb:(m + 1) * mb, :] = _silu(z)

    def conv_part(n, q, cin_ref):
        sl = n * spc + q
        wk = [jnp.broadcast_to(wdw_ref[sl, k:k + 1, :], (SUBLANES, LANES)) for k in range(CONV_K)]
        bias = jnp.broadcast_to(bdw_ref[sl], (SUBLANES, LANES))
        for grp in range(geo["n_grp"]):
            for ti in range(geo["n_t0"] // tb):
                base = PAD + grp * geo["grp_pitch"] + ti * tb
                accs = [bias] * tb
                for j in range(tb + CONV_K - 1):
                    xrow = cin_ref[q, pl.ds(base + (j - CONV_HALF), SUBLANES, stride=stride), :]
                    for u in range(max(0, j - CONV_K + 1), min(tb, j + 1)):
                        accs[u] = accs[u] + wk[j - u] * xrow
                for u in range(tb):
                    cout_ref[sl, pl.ds(base + u, SUBLANES, stride=stride), :] = accs[u]

    def proj_and_conv(n_proj, cin_proj, n_conv, cin_conv):
        for part in range(spc):
            proj_part(n_proj, part, cin_proj)
            conv_part(n_conv, part, cin_conv)

    for m in range(spc):
        proj_part(0, m, cina_ref)

    def pair_body(i, carry):
        n = 2 * i + 1
        proj_and_conv(n, cinb_ref, n - 1, cina_ref)
        proj_and_conv(n + 1, cina_ref, n, cinb_ref)
        return carry

    lax.fori_loop(0, nch // 2 - 1, pair_body, 0)
    proj_and_conv(nch - 1, cinb_ref, nch - 2, cina_ref)
    for q in range(spc):
        conv_part(nch - 1, q, cinb_ref)

    for (t, crow) in geo["pieces"]:
        rows = pl.ds(crow, ROW_CHUNK)
        s1 = cout_ref[0, rows, :]
        for sl in range(1, N_SLABS):
            s1 = s1 + cout_ref[sl, rows, :]
        mu = jnp.sum(s1, axis=-1, keepdims=True) * (1.0 / CONV_E)
        s2 = jnp.zeros((ROW_CHUNK, LANES), F32)
        for sl in range(N_SLABS):
            d = cout_ref[sl, rows, :] - mu
            s2 = s2 + d * d
        var = jnp.sum(s2, axis=-1, keepdims=True) * (1.0 / CONV_E)
        rstd = lax.rsqrt(var + EPS)
        for sl in range(N_SLABS):
            cs = slice(sl * LANES, (sl + 1) * LANES)
            yn = (cout_ref[sl, rows, :] - mu) * rstd * lng_ref[:, cs] + lnb_ref[:, cs]
            zq = zs_ref[sl * LANES // cw, t:t + ROW_CHUNK, (sl * LANES) % cw:(sl * LANES) % cw + LANES]
            m_ref[t:t + ROW_CHUNK, cs] = (_silu(yn) * zq).astype(BF16)

    gate = mod_ref[0, 0, 2:3, :]
    for m in range(tm // mb):
        rows = slice(m * mb, (m + 1) * mb)
        out = _dot(m_ref[rows, :], wout_ref[...])
        o_ref[rows, :] = x_ref[rows, :] + gate * out


def _conformer_layer(x2d, mods4, layer, cond_of_block, seg_len, g_row, win3, bin3, wdw_s, bdw_s, lng, lnb, wout):
    n_tok = x2d.shape[0]
    tm = 512
    cw = 256
    mb = 256
    geo = _conv_geometry(seg_len, tm)
    nch = CONV_E // cw
    const = dict(pipeline_mode=pl.Buffered(1))
    kern = functools.partial(_conformer_kernel, tm=tm, geo=geo, cw=cw, mb=mb)
    return pl.pallas_call(
        kern,
        grid=(n_tok // tm,),
        in_specs=[
            pl.BlockSpec((tm, D_MODEL), lambda i: (i, 0)),
            pl.BlockSpec((1, 1, 3, D_MODEL), lambda i: (layer, cond_of_block(i), 0, 0)),
            pl.BlockSpec((1, D_MODEL), lambda i: (0, 0)),
            pl.BlockSpec((3, nch, D_MODEL, cw), lambda i: (0, 0, 0, 0), **const),
            pl.BlockSpec((3, nch, 1, cw), lambda i: (0, 0, 0, 0)),
            pl.BlockSpec((N_SLABS, 32, LANES), lambda i: (0, 0, 0)),
            pl.BlockSpec((N_SLABS, 1, LANES), lambda i: (0, 0, 0)),
            pl.BlockSpec((1, CONV_E), lambda i: (0, 0)),
            pl.BlockSpec((1, CONV_E), lambda i: (0, 0)),
            pl.BlockSpec((CONV_E, D_MODEL), lambda i: (0, 0), **const),
        ],
        out_specs=pl.BlockSpec((tm, D_MODEL), lambda i: (i, 0)),
        out_shape=jax.ShapeDtypeStruct((n_tok, D_MODEL), F32),
        scratch_shapes=[
            pltpu.VMEM((tm, D_MODEL), BF16),
            pltpu.VMEM((cw // LANES, geo["rows"], LANES), F32),
            pltpu.VMEM((cw // LANES, geo["rows"], LANES), F32),
            pltpu.VMEM((N_SLABS, geo["rows"], LANES), F32),
            pltpu.VMEM((nch, tm, cw), F32),
            pltpu.VMEM((tm, CONV_E), BF16),
        ],
        compiler_params=pltpu.CompilerParams(dimension_semantics=("arbitrary",), vmem_limit_bytes=VMEM_LIMIT),
        name="conformer_seg%d" % seg_len,
    )(x2d, mods4, g_row, win3, bin3, wdw_s, bdw_s, lng, lnb, wout)


def _split3(x):
    hi = x.astype(BF16)
    r1 = x - hi.astype(F32)
    mid = r1.astype(BF16)
    lo = (r1 - mid.astype(F32)).astype(BF16)
    return hi, mid, lo


def _ssd_proj_kernel(x_ref, xp_ref, xn_ref, mod_ref, g_ref, wz_ref, wx_ref, wd_ref, wconv_ref, bconv_ref,
                     dtb_ref, arow_ref, z_ref, act_ref, dtc_ref, h_ref, exta_ref, extb_ref, asa_ref, asb_ref,
                     *, tm, blocks_per_seq, nw):
    i = pl.program_id(0)
    stride = (tm + 2 * HALO) // SUBLANES
    row0 = SUBLANES + HALO

    @pl.when(i == 0)
    def _():
        exta_ref[...] = jnp.zeros(exta_ref.shape, F32)
        extb_ref[...] = jnp.zeros(extb_ref.shape, F32)

    _modulated_norm_to(h_ref, x_ref, g_ref, mod_ref, tm)
    g = g_ref[...]
    sh = mod_ref[0, 0, 0:1, :]
    sc = mod_ref[0, 0, 1:2, :]
    pos = i % blocks_per_seq
    hp = jnp.where(pos > 0, _modulated_norm(xp_ref[...], g, sh, sc), 0.0)
    hn = jnp.where(pos < blocks_per_seq - 1, _modulated_norm(xn_ref[...], g, sh, sc), 0.0)
    h_ref[tm:tm + HALO, :] = hp.astype(BF16)
    h_ref[tm + HALO:tm + 2 * HALO, :] = hn.astype(BF16)

    spc = nw // LANES
    n_chunks = XBC // nw
    assert n_chunks % 2 == 0
    m_chunks = [(m0, min(m0 + 256, tm)) for m0 in range(0, tm, 256)]
    m_chunks[-1] = (m_chunks[-1][0], tm + 2 * HALO)
    n_trips = 4
    tb = stride // n_trips

    def xproj(j, ext_ref):
        for (m0, m1) in m_chunks:
            nb = min(m1, tm) - m0
            r = _dot(h_ref[m0:m1, :], wx_ref[j])
            for q in range(spc):
                ls = slice(q * LANES, (q + 1) * LANES)
                ext_ref[q, row0 + m0:row0 + m0 + nb, :] = r[0:nb, ls]
                if m1 > tm:
                    ext_ref[q, SUBLANES:SUBLANES + HALO, :] = r[nb:nb + HALO, ls]
                    ext_ref[q, row0 + tm:row0 + tm + HALO, :] = r[nb + HALO:nb + 2 * HALO, ls]

    def conv(j, ext_ref, as_ref):
        for q in range(spc):
            sl = j * spc + q
            wk = [jnp.broadcast_to(wconv_ref[sl, k:k + 1, :], (SUBLANES, LANES)) for k in range(SSD_CONV_K)]
            bias = jnp.broadcast_to(bconv_ref[sl], (SUBLANES, LANES))
            for ti in range(n_trips):
                base = SUBLANES + ti * tb
                accs = [bias] * tb
                for jj in range(tb + SSD_CONV_K - 1):
                    xrow = ext_ref[q, pl.ds(base + (jj - SSD_CONV_HALF), SUBLANES, stride=stride), :]
                    for u in range(max(0, jj - SSD_CONV_K + 1), min(tb, jj + 1)):
                        accs[u] = accs[u] + wk[jj - u] * xrow
                for u in range(tb):
                    as_ref[q, pl.ds(base + u, SUBLANES, stride=stride), :] = _silu(accs[u])
            act_ref[sl] = as_ref[q, row0:row0 + tm, :]

    xproj(0, exta_ref)

    def pair_body(p, carry):
        j = 2 * p + 1
        xproj(j, extb_ref)
        conv(j - 1, exta_ref, asa_ref)
        xproj(j + 1, exta_ref)
        conv(j, extb_ref, asb_ref)
        return carry

    lax.fori_loop(0, n_chunks // 2 - 1, pair_body, 0)
    xproj(n_chunks - 1, extb_ref)
    conv(n_chunks - 2, exta_ref, asa_ref)

    conv(n_chunks - 1, extb_ref, asb_ref)
    for m in range(tm // 256):
        rows = slice(m * 256, (m + 1) * 256)
        hm = h_ref[rows, :]
        for j in range(SSD_INNER // nw):
            cs = slice(j * nw, (j + 1) * nw)
            z_ref[rows, cs] = _dot(hm, wz_ref[:, cs])

    t = CHUNK
    row = lax.broadcasted_iota(jnp.int32, (t, t), 0)
    col = lax.broadcasted_iota(jnp.int32, (t, t), 1)
    lmat = jnp.where(col <= row, 1.0, 0.0).astype(BF16)
    umat = jnp.where(col >= row, 1.0, 0.0).astype(BF16)
    for j in range(tm // t):
        rows = slice(j * t, (j + 1) * t)
        dt_all = _softplus(_dot(h_ref[rows, :], wd_ref[...]) + dtb_ref[...])
        parts = _split3(dt_all * arow_ref[...])
        cum_f = _dot(lmat, parts[0]) + _dot(lmat, parts[1]) + _dot(lmat, parts[2])
        cum_b = _dot(umat, parts[0]) + _dot(umat, parts[1]) + _dot(umat, parts[2])
        cum = jnp.where(col < SSD_HEADS, cum_f, cum_b)
        cum_t = cum.T
        dt_t = dt_all.T
        edge = jnp.where(row < SSD_HEADS, cum_t[:, t - 1:t], cum_t[:, 0:1])
        dtc_ref[0, rows, :] = cum
        dtc_ref[1, rows, :] = cum_t
        dtc_ref[2, rows, :] = dt_t
        dtc_ref[3, rows, :] = dt_t * jnp.exp(edge - cum_t)


def _ssd_proj(x2d, mods4, layer, cond_of_block, seq_len, g_row, wz, wx, wd, wconv_s, bconv_s, dtb_row, a_row):
    n_tok = x2d.shape[0]
    tm = min(512, seq_len)
    bps = seq_len // tm
    hb = tm // HALO
    last_hb = n_tok // HALO - 1
    rows = tm + 2 * HALO + 2 * SUBLANES
    const = dict(pipeline_mode=pl.Buffered(1))
    nw = wx.shape[-1]
    kern = functools.partial(_ssd_proj_kernel, tm=tm, blocks_per_seq=bps, nw=nw)
    return pl.pallas_call(
        kern,
        grid=(n_tok // tm,),
        in_specs=[
            pl.BlockSpec((tm, D_MODEL), lambda i: (i, 0)),
            pl.BlockSpec((HALO, D_MODEL), lambda i: (jnp.maximum(i * hb - 1, 0), 0)),
            pl.BlockSpec((HALO, D_MODEL), lambda i: (jnp.minimum((i + 1) * hb, last_hb), 0)),
            pl.BlockSpec((1, 1, 3, D_MODEL), lambda i: (layer, cond_of_block(i), 0, 0)),
            pl.BlockSpec((1, D_MODEL), lambda i: (0, 0)),
            pl.BlockSpec((D_MODEL, SSD_INNER), lambda i: (0, 0), **const),
            pl.BlockSpec((XBC // nw, D_MODEL, nw), lambda i: (0, 0, 0), **const),
            pl.BlockSpec((D_MODEL, LANES), lambda i: (0, 0), **const),
            pl.BlockSpec((XBC_SLABS, SUBLANES, LANES), lambda i: (0, 0, 0)),
            pl.BlockSpec((XBC_SLABS, 1, LANES), lambda i: (0, 0, 0)),
            pl.BlockSpec((1, LANES), lambda i: (0, 0)),
            pl.BlockSpec((1, LANES), lambda i: (0, 0)),
        ],
        out_specs=[
            pl.BlockSpec((tm, SSD_INNER), lambda i: (i, 0)),
            pl.BlockSpec((XBC_SLABS, tm, LANES), lambda i: (0, i, 0)),
            pl.BlockSpec((4, tm, LANES), lambda i: (0, i, 0)),
        ],
        out_shape=[
            jax.ShapeDtypeStruct((n_tok, SSD_INNER), F32),
            jax.ShapeDtypeStruct((XBC_SLABS, n_tok, LANES), F32),
            jax.ShapeDtypeStruct((4, n_tok, LANES), F32),
        ],
        scratch_shapes=[
            pltpu.VMEM((tm + 2 * HALO, D_MODEL), BF16),
            pltpu.VMEM((nw // LANES, rows, LANES), F32),
            pltpu.VMEM((nw // LANES, rows, LANES), F32),
            pltpu.VMEM((nw // LANES, rows, LANES), F32),
            pltpu.VMEM((nw // LANES, rows, LANES), F32),
        ],
        compiler_params=pltpu.CompilerParams(dimension_semantics=("arbitrary",), vmem_limit_bytes=VMEM_LIMIT),
        name="ssd_proj_L%d" % seq_len,
    )(x2d, x2d, x2d, mods4, g_row, wz, wx, wd, wconv_s, bconv_s, dtb_row, a_row)


def _ssd_scan_kernel(*refs, n_steps, sub_chunks, has_h0, emit_state):
    it = iter(refs)
    act_ref, dtc_ref, z_ref, x_ref, mod_ref, fing_ref, dexp_ref, ng_ref, wout_ref = (next(it) for _ in range(9))
    if has_h0:
        h0f_ref, h0b_ref = next(it), next(it)
    o_ref = next(it)
    if emit_state:
        sf_ref, sb_ref = next(it), next(it)
    yb_ref, ht_ref, gt_ref, y_ref, m_ref = (next(it) for _ in range(5))

    t = CHUNK
    s = pl.program_id(1)
    c = pl.program_id(2)
    bi = s * c + (1 - s) * (n_steps - 1 - c)

    row = lax.broadcasted_iota(jnp.int32, (t, t), 0)
    col = lax.broadcasted_iota(jnp.int32, (t, t), 1)
    lower = col <= row
    upper = col >= row
    lane_lo = col < SSD_HEAD_DIM
    cum_k, cumt_k, dtt_k, w1t_k = 0, 1, 2, 3

    def colb(r0, j):
        return jnp.broadcast_to(dtc_ref[cum_k, r0:r0 + t, j:j + 1], (t, LANES))

    def rowb(r0, k, j):
        return jnp.broadcast_to(dtc_ref[k, r0 + j:r0 + j + 1, :], (t, t))

    def pair_rhs(xs_pair):
        return jnp.concatenate([jnp.where(lane_lo, xs_pair, 0.0).astype(BF16),
                                jnp.where(lane_lo, 0.0, xs_pair).astype(BF16)], axis=0)

    def state_lhs(b_t, r0, j0):
        return jnp.concatenate([(b_t * rowb(r0, w1t_k, j0)).astype(BF16),
                                (b_t * rowb(r0, w1t_k, j0 + 1)).astype(BF16)], axis=1)

    hp_per_g = SSD_HEADS // SSD_GROUPS // 2

    def bwd_chunk(sub):
        r0 = sub * t
        rows = slice(r0, r0 + t)
        ci = bi * sub_chunks + sub
        for g in range(SSD_GROUPS):
            gs = slice(g * 512, (g + 1) * 512)
            c_g = act_ref[C_SLAB0 + g, rows, :].astype(BF16)
            b_t = act_ref[B_SLAB0 + g, rows, :].T
            yoff = _dot(c_g, gt_ref[:, gs].astype(BF16))
            for q in range(hp_per_g):
                hp = g * hp_per_g + q
                ps = slice(hp * LANES, (hp + 1) * LANES)
                j0 = SSD_HEADS + 2 * hp
                r_pair = jnp.where(lane_lo, colb(r0, j0), colb(r0, j0 + 1))
                yb_ref[ci, :, ps] = (yoff[:, q * LANES:(q + 1) * LANES] * jnp.exp(r_pair)).astype(BF16)
                upd = _dot(state_lhs(b_t, r0, j0), pair_rhs(act_ref[hp, rows, :]))
                gt_ref[:, ps] = gt_ref[:, ps] * jnp.exp(r_pair[0:1, :]) + upd

    def fwd_chunk(sub):
        r0 = sub * t
        rows = slice(r0, r0 + t)
        ci = bi * sub_chunks + sub
        for g in range(SSD_GROUPS):
            gs = slice(g * 512, (g + 1) * 512)
            b_f32 = act_ref[B_SLAB0 + g, rows, :]
            b_t = b_f32.T
            c_g = act_ref[C_SLAB0 + g, rows, :].astype(BF16)
            cb = lax.dot_general(c_g, b_f32.astype(BF16), (((1,), (1,)), ((), ())), preferred_element_type=F32)
            y_ref[rows, gs] = _dot(c_g, ht_ref[:, gs].astype(BF16))
            for q in range(hp_per_g):
                hp = g * hp_per_g + q
                ps = slice(hp * LANES, (hp + 1) * LANES)
                lhs = []
                col_f = []
                for e in range(2):
                    h = 2 * hp + e
                    hb = SSD_HEADS + h
                    cf = colb(r0, h)
                    col_f.append(cf)
                    dec_f = jnp.where(lower, jnp.exp(cf - rowb(r0, cumt_k, h)), 0.0) * rowb(r0, dtt_k, h)
                    dec_b = (jnp.where(upper, jnp.exp(colb(r0, hb) - rowb(r0, cumt_k, hb)), 0.0)
                             * rowb(r0, dtt_k, hb))
                    lhs.append((cb * (dec_f + dec_b)).astype(BF16))
                xs_pair = act_ref[hp, rows, :]
                rhs = pair_rhs(xs_pair)
                y_diag = _dot(jnp.concatenate(lhs, axis=1), rhs)
                cf_pair = jnp.where(lane_lo, col_f[0], col_f[1])
                y_ref[rows, ps] = (y_diag + y_ref[rows, ps] * jnp.exp(cf_pair)
                                   + yb_ref[ci, :, ps].astype(F32) + dexp_ref[:, ps] * xs_pair)
                upd = _dot(state_lhs(b_t, r0, 2 * hp), rhs)
                ht_ref[:, ps] = ht_ref[:, ps] * jnp.exp(cf_pair[t - 1:t, :]) + upd

    def gate_and_project():
        gw = SSD_INNER // SSD_GROUPS
        for sub in range(sub_chunks):
            rows = slice(sub * t, (sub + 1) * t)
            for k in range(SSD_GROUPS):
                ks = slice(k * gw, (k + 1) * gw)
                yg = y_ref[rows, ks] * _silu(z_ref[rows, ks])
                ms = jnp.mean(yg * yg, axis=-1, keepdims=True)
                m_ref[rows, ks] = (yg * lax.rsqrt(ms + EPS) * ng_ref[:, ks]).astype(BF16)
        out = _dot(m_ref[...], wout_ref[...])
        ynew = x_ref[...] + mod_ref[0, 0, 2:3, :] * out
        ms = jnp.mean(ynew * ynew, axis=-1, keepdims=True)
        o_ref[...] = ynew * lax.rsqrt(ms + EPS) * fing_ref[...]

    @pl.when(s == 0)
    def _():
        @pl.when(c == 0)
        def _():
            if has_h0:
                gt_ref[...] = h0b_ref[0].T
            else:
                gt_ref[...] = jnp.zeros(gt_ref.shape, F32)

        for sub in reversed(range(sub_chunks)):
            bwd_chunk(sub)

        if emit_state:
            @pl.when(c == n_steps - 1)
            def _():
                sb_ref[0] = gt_ref[...].T

    @pl.when(s == 1)
    def _():
        @pl.when(c == 0)
        def _():
            if has_h0:
                ht_ref[...] = h0f_ref[0].T
            else:
                ht_ref[...] = jnp.zeros(ht_ref.shape, F32)

        for sub in range(sub_chunks):
            fwd_chunk(sub)
        gate_and_project()

        if emit_state:
            @pl.when(c == n_steps - 1)
            def _():
                sf_ref[0] = ht_ref[...].T


def _ssd_scan(x2d, z, act, dtc, mods4, layer, cond_of_batch, seq_len, fin_g, d_exp, ng_row, wout, h0f, h0b,
              emit_state):
    n_tok = x2d.shape[0]
    n_batch = n_tok // seq_len
    nc = seq_len // CHUNK
    sub = min(4, nc)
    ns = nc // sub
    tr = sub * CHUNK
    has_h0 = h0f is not None

    def blk(b, s, c):
        return b * ns + s * c + (1 - s) * (ns - 1 - c)

    const = dict(pipeline_mode=pl.Buffered(1))
    hp = SSD_HEADS * SSD_HEAD_DIM
    in_specs = [
        pl.BlockSpec((XBC_SLABS, tr, LANES), lambda b, s, c: (0, blk(b, s, c), 0)),
        pl.BlockSpec((4, tr, LANES), lambda b, s, c: (0, blk(b, s, c), 0)),
        pl.BlockSpec((tr, SSD_INNER), lambda b, s, c: (b * ns + s * c, 0)),
        pl.BlockSpec((tr, D_MODEL), lambda b, s, c: (b * ns + s * c, 0)),
        pl.BlockSpec((1, 1, 3, D_MODEL), lambda b, s, c: (layer, cond_of_batch(b), 0, 0)),
        pl.BlockSpec((1, D_MODEL), lambda b, s, c: (0, 0)),
        pl.BlockSpec((1, SSD_INNER), lambda b, s, c: (0, 0)),
        pl.BlockSpec((1, SSD_INNER), lambda b, s, c: (0, 0)),
        pl.BlockSpec((SSD_INNER, D_MODEL), lambda b, s, c: (0, 0), **const),
    ]
    args = [act, dtc, z, x2d, mods4, fin_g, d_exp, ng_row, wout]
    if has_h0:
        in_specs += [pl.BlockSpec((1, hp, D_STATE), lambda b, s, c: (b, 0, 0), **const)] * 2
        args += [h0f, h0b]
    out_specs = [pl.BlockSpec((tr, D_MODEL), lambda b, s, c: (b * ns + s * c, 0))]
    out_shape = [jax.ShapeDtypeStruct((n_tok, D_MODEL), F32)]
    if emit_state:
        out_specs += [pl.BlockSpec((1, hp, D_STATE), lambda b, s, c: (b, 0, 0))] * 2
        out_shape += [jax.ShapeDtypeStruct((n_batch, hp, D_STATE), F32)] * 2
    kern = functools.partial(_ssd_scan_kernel, n_steps=ns, sub_chunks=sub, has_h0=has_h0, emit_state=emit_state)
    return pl.pallas_call(
        kern,
        grid=(n_batch, 2, ns),
        in_specs=in_specs,
        out_specs=out_specs,
        out_shape=out_shape,
        scratch_shapes=[
            pltpu.VMEM((nc, CHUNK, SSD_INNER), BF16),
            pltpu.VMEM((D_STATE, hp), F32),
            pltpu.VMEM((D_STATE, hp), F32),
            pltpu.VMEM((tr, SSD_INNER), F32),
            pltpu.VMEM((tr, SSD_INNER), BF16),
        ],
        compiler_params=pltpu.CompilerParams(dimension_semantics=("arbitrary", "arbitrary", "arbitrary"),
                                             vmem_limit_bytes=VMEM_LIMIT),
        name="ssd_scan_L%d" % seq_len,
    )(*args)


def kernel(x_prompt, x_sample, state_fwd, state_bwd, c, c_ctx, ada_w, ada_b, norm_g, conv_w_in, conv_b_in, conv_w_dw, conv_b_dw, conv_ln_g, conv_ln_b, conv_w_out, ssd_w_in, ssd_w_conv, ssd_b_conv, ssd_dt_bias_f, ssd_dt_bias_b, ssd_a_log_f, ssd_a_log_b, ssd_d, ssd_norm_g, ssd_w_out, final_norm_g):
    nb_p, seq_p, _ = x_prompt.shape
    nb_s, seq_s, _ = x_sample.shape
    grid_w = 64

    cond = jnp.concatenate([c_ctx[None, :], c, jnp.zeros((16 - 1 - nb_s, D_MODEL), F32)], axis=0)
    mods4 = _adaln(cond, ada_w, ada_b).reshape(ada_w.shape[0], 16, 3, D_MODEL)

    xp = x_prompt.reshape(nb_p * seq_p, D_MODEL)
    xs = x_sample.reshape(nb_s * seq_s, D_MODEL)

    cw = 256
    nch = CONV_E // cw
    win3 = conv_w_in[0].reshape(D_MODEL, 3, nch, cw).transpose(1, 2, 0, 3).astype(BF16)
    bin3 = conv_b_in[0].reshape(3, nch, 1, cw)
    wdw_s = jnp.pad(conv_w_dw[0], ((0, 32 - CONV_K), (0, 0))).reshape(32, N_SLABS, LANES).transpose(1, 0, 2)
    bdw_s = conv_b_dw[0].reshape(N_SLABS, 1, LANES)
    lng = conv_ln_g[0].reshape(1, CONV_E)
    lnb = conv_ln_b[0].reshape(1, CONV_E)
    wout0 = conv_w_out[0].astype(BF16)
    g0 = norm_g[0].reshape(1, D_MODEL)
    tm = 512
    xp = _conformer_layer(xp, mods4, 0, lambda i: 0, seq_p, g0, win3, bin3, wdw_s, bdw_s, lng, lnb, wout0)
    xs = _conformer_layer(xs, mods4, 0, lambda i: 1 + i // (seq_s // tm), grid_w, g0, win3, bin3, wdw_s, bdw_s,
                          lng, lnb, wout0)

    w_in = ssd_w_in[0]
    wz = w_in[:, :SSD_INNER].astype(BF16)
    nw = 512
    wx = w_in[:, SSD_INNER:SSD_INNER + XBC].reshape(D_MODEL, XBC // nw, nw).transpose(1, 0, 2).astype(BF16)
    wd = jnp.pad(w_in[:, SSD_INNER + XBC:], ((0, 0), (0, LANES - 2 * SSD_HEADS))).astype(BF16)
    g1 = norm_g[1].reshape(1, D_MODEL)
    wconv_s = jnp.pad(ssd_w_conv[0], ((0, SUBLANES - SSD_CONV_K), (0, 0))).reshape(
        SUBLANES, XBC_SLABS, LANES).transpose(1, 0, 2)
    bconv_s = ssd_b_conv[0].reshape(XBC_SLABS, 1, LANES)
    zpad = jnp.zeros((LANES - 2 * SSD_HEADS,), F32)
    dtb_row = jnp.concatenate([ssd_dt_bias_f[0], ssd_dt_bias_b[0], zpad]).reshape(1, LANES)
    a_row = jnp.concatenate([-jnp.exp(ssd_a_log_f[0]), -jnp.exp(ssd_a_log_b[0]), zpad]).reshape(1, LANES)
    d_exp = jnp.repeat(ssd_d[0], SSD_HEAD_DIM).reshape(1, SSD_INNER)
    ng_row = ssd_norm_g[0].reshape(1, SSD_INNER)
    wout1 = ssd_w_out[0].astype(BF16)
    fin_g = final_norm_g.reshape(1, D_MODEL)
    hp = SSD_HEADS * SSD_HEAD_DIM

    zp, actp, dtcp = _ssd_proj(xp, mods4, 1, lambda i: 0, seq_p, g1, wz, wx, wd, wconv_s, bconv_s, dtb_row, a_row)
    zs_, acts, dtcs = _ssd_proj(xs, mods4, 1, lambda i: 1 + i // (seq_s // tm), seq_s, g1, wz, wx, wd, wconv_s,
                                bconv_s, dtb_row, a_row)

    yp, sf, sb = _ssd_scan(xp, zp, actp, dtcp, mods4, 1, lambda b: 0, seq_p, fin_g, d_exp, ng_row, wout1,
                           None, None, True)
    (ys,) = _ssd_scan(xs, zs_, acts, dtcs, mods4, 1, lambda b: 1 + b, seq_s, fin_g, d_exp, ng_row, wout1,
                      state_fwd[:, 0].reshape(nb_s, hp, D_STATE), state_bwd[:, 0].reshape(nb_s, hp, D_STATE), False)

    y_prompt = yp.reshape(nb_p, seq_p, D_MODEL)
    y_sample = ys.reshape(nb_s, seq_s, D_MODEL)
    new_f = sf.reshape(nb_p, 1, SSD_HEADS, SSD_HEAD_DIM, D_STATE)
    new_b = sb.reshape(nb_p, 1, SSD_HEADS, SSD_HEAD_DIM, D_STATE)
    return (y_prompt, y_sample, new_f, new_b)
```
